```python
import jax, jax.numpy as jnp
from jax import lax
import numpy as np

D_MODEL = 1024
BATCH = 8
SEQ = 2048
DEPTH = 4

ATTN_HEADS = 8
ATTN_KV_HEADS = 2
HEAD_DIM = 64
WINDOW = 128
BLOCK = 128
DN_HEADS = 8
DN_DK = 64
DN_DV = 64
DN_CONV = 4
CHUNK = 64
CONV_DIM = D_MODEL
CONV_WIDTH = 31
D_FF = 2816
EPS = 1e-6

Q_A = ATTN_HEADS * HEAD_DIM
KV_A = ATTN_KV_HEADS * HEAD_DIM
QK_B = DN_HEADS * DN_DK
V_B = DN_HEADS * DN_DV
QKV_B = 2 * QK_B + V_B
IN_SPLIT_SIZES = (Q_A, KV_A, KV_A, QKV_B, V_B, DN_HEADS, DN_HEADS)
IN_COLS = sum(IN_SPLIT_SIZES)
MIX_WIDTH = Q_A + V_B
N_EVEN = (DEPTH + 1) // 2
N_ODD = DEPTH // 2

kernel_name = "hybrid_swa_deltanet_conformer_macaron"


def rmsnorm(x, w):
    xf = x.astype(jnp.float32)
    y = xf * lax.rsqrt(jnp.mean(xf * xf, axis=-1, keepdims=True) + EPS)
    return (y * w.astype(jnp.float32)).astype(x.dtype)


def layernorm(x, w, b):
    xf = x.astype(jnp.float32)
    mu = jnp.mean(xf, axis=-1, keepdims=True)
    xc = xf - mu
    y = xc * lax.rsqrt(jnp.mean(xc * xc, axis=-1, keepdims=True) + EPS)
    return (y * w.astype(jnp.float32) + b.astype(jnp.float32)).astype(x.dtype)


def l2norm(x):
    xf = x.astype(jnp.float32)
    return xf * lax.rsqrt(jnp.sum(xf * xf, axis=-1, keepdims=True) + EPS)


def causal_depthwise_conv(x, w):
    k_width = w.shape[0]
    return lax.conv_general_dilated(
        x, w[:, None, :].astype(x.dtype), window_strides=(1,), padding=[(k_width - 1, 0)],
        dimension_numbers=('NWC', 'WIO', 'NWC'), feature_group_count=x.shape[-1])


def swiglu(x, w_gate, w_up, w_down):
    return (jax.nn.silu(x @ w_gate) * (x @ w_up)) @ w_down


def alibi_slopes(n_heads):
    return jnp.asarray(2.0 ** (-8.0 * np.arange(1, n_heads + 1) / n_heads), dtype=jnp.float32)


def sliding_window_attention(q, k, v, sinks):
    B, T, Hq, d = q.shape
    Hkv = k.shape[2]
    G = Hq // Hkv
    N = T // BLOCK
    qb = q.reshape(B, N, BLOCK, Hkv, G, d)

    def with_prev(t):
        tb = t.reshape(B, N, BLOCK, Hkv, d)
        prev = jnp.pad(tb, ((0, 0), (1, 0), (0, 0), (0, 0), (0, 0)))[:, :-1]
        return jnp.concatenate([prev, tb], axis=2)

    kb, vb = with_prev(k), with_prev(v)
    s = jnp.einsum('bnikgd,bnjkd->bkgnij', qb, kb).astype(jnp.float32) * (d ** -0.5)
    i = jnp.arange(BLOCK)[:, None]
    j = jnp.arange(2 * BLOCK)[None, :]
    dist = i + BLOCK - j
    blk = jnp.arange(N)[:, None, None]
    valid = (dist >= 0) & (dist < WINDOW) & ((blk > 0) | (j >= BLOCK))
    slopes = alibi_slopes(Hq).reshape(Hkv, G)[:, :, None, None, None]
    s = s - slopes * dist.astype(jnp.float32)
    s = jnp.where(valid, s, -1e30)
    sink = sinks.astype(jnp.float32).reshape(Hkv, G)[:, :, None, None, None]
    m = jnp.maximum(jnp.max(s, axis=-1, keepdims=True), sink)
    e = jnp.exp(s - m)
    p = e / (jnp.sum(e, axis=-1, keepdims=True) + jnp.exp(sink - m))
    o = jnp.einsum('bkgnij,bnjkd->bnikgd', p.astype(v.dtype), vb)
    return o.reshape(B, T, Hq * d)


def gated_delta_rule_chunked(q, k, v, g, beta):
    B, T, H, dk = q.shape
    dv = v.shape[-1]
    N = T // CHUNK
    f32 = jnp.float32

    def chunks(t):
        return t.astype(f32).reshape(B, N, CHUNK, H, -1).transpose(0, 3, 1, 2, 4)

    def chunks_s(t):
        return t.astype(f32).reshape(B, N, CHUNK, H).transpose(0, 3, 1, 2)

    q = chunks(q) * (dk ** -0.5)
    k, v = chunks(k), chunks(v)
    g, beta = chunks_s(g), chunks_s(beta)
    gc = jnp.cumsum(g, axis=-1)
    causal = jnp.tril(jnp.ones((CHUNK, CHUNK), dtype=bool))
    strict = jnp.tril(jnp.ones((CHUNK, CHUNK), dtype=bool), -1)
    diff = gc[..., :, None] - gc[..., None, :]
    decay = jnp.where(causal, jnp.exp(jnp.where(causal, diff, 0.0)), 0.0)
    kb = k * beta[..., None]
    low = jnp.where(strict, jnp.einsum('bhnid,bhnjd->bhnij', kb, k) * decay, 0.0)
    rhs = jnp.concatenate([v * beta[..., None], kb * jnp.exp(gc)[..., None]], axis=-1)
    sol = lax.linalg.triangular_solve(low, rhs, left_side=True, lower=True, unit_diagonal=True)
    u, w = sol[..., :dv], sol[..., dv:]
    attn = jnp.einsum('bhnid,bhnjd->bhnij', q, k) * decay
    q_dec = q * jnp.exp(gc)[..., None]
    k_dec = k * jnp.exp(gc[..., -1:] - gc)[..., None]
    g_last = jnp.exp(gc[..., -1])

    def step(S, xs):
        u_n, w_n, attn_n, qd_n, kd_n, gl_n = xs
        v_new = u_n - jnp.einsum('bhcd,bhde->bhce', w_n, S)
        o_n = jnp.einsum('bhcd,bhde->bhce', qd_n, S) + jnp.einsum('bhij,bhje->bhie', attn_n, v_new)
        S = S * gl_n[..., None, None] + jnp.einsum('bhcd,bhce->bhde', kd_n, v_new)
        return S, o_n

    xs = tuple(jnp.moveaxis(t, 2, 0) for t in (u, w, attn, q_dec, k_dec, g_last))
    S0 = jnp.zeros((B, H, dk, dv), f32)
    _, o = lax.scan(step, S0, xs)
    return o.transpose(1, 0, 3, 2, 4).reshape(B, T, H, dv)


def attn_deltanet_mixer(h, w_in, dn_conv_w, attn_sinks, dn_a_log, dn_dt_bias, dn_norm_w, w_out):
    B, T, _ = h.shape
    proj = h @ w_in
    split_idx = list(np.cumsum(IN_SPLIT_SIZES)[:-1])
    qa, ka, va, qkv_b, z, b_raw, a_raw = jnp.split(proj, split_idx, axis=-1)
    att = sliding_window_attention(qa.reshape(B, T, ATTN_HEADS, HEAD_DIM),
                                   ka.reshape(B, T, ATTN_KV_HEADS, HEAD_DIM),
                                   va.reshape(B, T, ATTN_KV_HEADS, HEAD_DIM), attn_sinks)
    qkv_b = jax.nn.silu(causal_depthwise_conv(qkv_b, dn_conv_w))
    qb, kb, vb = jnp.split(qkv_b, [QK_B, 2 * QK_B], axis=-1)
    qb = l2norm(qb.reshape(B, T, DN_HEADS, DN_DK))
    kb = l2norm(kb.reshape(B, T, DN_HEADS, DN_DK))
    vb = vb.reshape(B, T, DN_HEADS, DN_DV)
    beta = jax.nn.sigmoid(b_raw.astype(jnp.float32))
    g = -jnp.exp(dn_a_log.astype(jnp.float32)) * jax.nn.softplus(
        a_raw.astype(jnp.float32) + dn_dt_bias.astype(jnp.float32))
    o = gated_delta_rule_chunked(qb, kb, vb, g, beta)
    o = rmsnorm(o, dn_norm_w) * jax.nn.silu(z.reshape(B, T, DN_HEADS, DN_DV).astype(jnp.float32))
    mix = jnp.concatenate([att, o.reshape(B, T, V_B).astype(h.dtype)], axis=-1)
    return mix @ w_out


def conformer_conv_module(h, w_pw1, b_pw1, w_dw, b_dw, ln_w, ln_b, w_pw2, b_pw2):
    u = h @ w_pw1 + b_pw1
    u = u[..., :CONV_DIM] * jax.nn.sigmoid(u[..., CONV_DIM:])
    u = causal_depthwise_conv(u, w_dw) + b_dw
    u = jax.nn.silu(layernorm(u, ln_w, ln_b))
    return u @ w_pw2 + b_pw2


def setup_inputs(seed: int = 0) -> dict:
    key = jax.random.key(seed)
    ks = jax.random.split(key, 24)
    f32 = jnp.float32

    def nrm(k, shape, scale):
        return jax.random.normal(k, shape, f32) * scale

    dt = jnp.exp(jax.random.uniform(ks[10], (N_EVEN, DN_HEADS), f32,
                                    np.log(1e-3), np.log(1e-1)))
    return {
        "x": nrm(ks[0], (BATCH, SEQ, D_MODEL), 1.0),
        "norm_w": 1.0 + nrm(ks[1], (DEPTH, 3, D_MODEL), 0.02),
        "ffn_w_gate": nrm(ks[2], (DEPTH, 2, D_MODEL, D_FF), D_MODEL ** -0.5),
        "ffn_w_up": nrm(ks[3], (DEPTH, 2, D_MODEL, D_FF), D_MODEL ** -0.5),
        "ffn_w_down": nrm(ks[4], (DEPTH, 2, D_FF, D_MODEL), D_FF ** -0.5),
        "mix_w_in": nrm(ks[5], (N_EVEN, D_MODEL, IN_COLS), D_MODEL ** -0.5),
        "dn_conv_w": nrm(ks[6], (N_EVEN, DN_CONV, QKV_B), DN_CONV ** -0.5),
        "attn_sinks": nrm(ks[7], (N_EVEN, ATTN_HEADS), 0.5),
        "dn_a_log": jnp.log(jax.random.uniform(ks[8], (N_EVEN, DN_HEADS), f32, 1.0, 16.0)),
        "dn_dt_bias": dt + jnp.log(-jnp.expm1(-dt)),
        "dn_norm_w": 1.0 + nrm(ks[9], (N_EVEN, DN_DV), 0.02),
        "mix_w_out": nrm(ks[11], (N_EVEN, MIX_WIDTH, D_MODEL), MIX_WIDTH ** -0.5),
        "conv_w_pw1": nrm(ks[12], (N_ODD, D_MODEL, 2 * CONV_DIM), D_MODEL ** -0.5),
        "conv_b_pw1": nrm(ks[13], (N_ODD, 2 * CONV_DIM), 0.02),
        "conv_w_dw": nrm(ks[14], (N_ODD, CONV_WIDTH, CONV_DIM), CONV_WIDTH ** -0.5),
        "conv_b_dw": nrm(ks[15], (N_ODD, CONV_DIM), 0.02),
        "conv_ln_w": 1.0 + nrm(ks[16], (N_ODD, CONV_DIM), 0.02),
        "conv_ln_b": nrm(ks[17], (N_ODD, CONV_DIM), 0.02),
        "conv_w_pw2": nrm(ks[18], (N_ODD, CONV_DIM, D_MODEL), CONV_DIM ** -0.5),
        "conv_b_pw2": nrm(ks[19], (N_ODD, D_MODEL), 0.02),
        "final_norm_w": 1.0 + nrm(ks[20], (D_MODEL,), 0.02),
    }


def reference(x, norm_w, ffn_w_gate, ffn_w_up, ffn_w_down, mix_w_in, dn_conv_w, attn_sinks,
              dn_a_log, dn_dt_bias, dn_norm_w, mix_w_out, conv_w_pw1, conv_b_pw1, conv_w_dw,
              conv_b_dw, conv_ln_w, conv_ln_b, conv_w_pw2, conv_b_pw2, final_norm_w):
    for layer in range(DEPTH):
        x = x + 0.5 * swiglu(rmsnorm(x, norm_w[layer, 0]),
                             ffn_w_gate[layer, 0], ffn_w_up[layer, 0], ffn_w_down[layer, 0])
        h = rmsnorm(x, norm_w[layer, 1])
        if layer % 2 == 0:
            e = layer // 2
            x = x + attn_deltanet_mixer(h, mix_w_in[e], dn_conv_w[e], attn_sinks[e], dn_a_log[e],
                                        dn_dt_bias[e], dn_norm_w[e], mix_w_out[e])
        else:
            c = layer // 2
            x = x + conformer_conv_module(h, conv_w_pw1[c], conv_b_pw1[c], conv_w_dw[c], conv_b_dw[c],
                                          conv_ln_w[c], conv_ln_b[c], conv_w_pw2[c], conv_b_pw2[c])
        x = x + 0.5 * swiglu(rmsnorm(x, norm_w[layer, 2]),
                             ffn_w_gate[layer, 1], ffn_w_up[layer, 1], ffn_w_down[layer, 1])
    return rmsnorm(x, final_norm_w)
```

```python
import functools

import numpy as np
import jax
import jax.numpy as jnp
from jax import lax
from jax.experimental import pallas as pl
from jax.experimental.pallas import tpu as pltpu

F32 = jnp.float32
BF16 = jnp.bfloat16
HIGHEST = lax.Precision.HIGHEST

ATTN_HEADS = 8
ATTN_KV_HEADS = 2
HEAD_DIM = 64
WINDOW = 128
ATTN_BLOCK = 128
DN_HEADS = 8
DN_DK = 64
DN_DV = 64
DN_CONV = 4
CHUNK = 64
EPS = 1e-6

V7X_LANES = 128
V7X_SUBLANES = 8
V7X_VMEM_LIMIT_BYTES = 56 * 1024 * 1024

ROW_TILE = 512
FFN_COL_TILE = 256
SEQ_TILE = 256
GATE_COLS = V7X_LANES


def _params(*sem):
    return pltpu.CompilerParams(dimension_semantics=sem, vmem_limit_bytes=V7X_VMEM_LIMIT_BYTES)


def _rms(x, w):
    return x * lax.rsqrt(jnp.mean(x * x, axis=-1, keepdims=True) + EPS) * w


def _silu(x):
    return x * (1.0 / (1.0 + jnp.exp(-x)))


def _softplus(x):
    return jnp.maximum(x, 0.0) + jnp.log(1.0 + jnp.exp(-jnp.abs(x)))


def _dot(a, b):
    return jnp.dot(a, b, preferred_element_type=F32)


def _resident(shape):
    return pl.BlockSpec(shape, lambda *_: (0,) * len(shape))


def _ffn_kernel(x_ref, nw_ref, wg_ref, wu_ref, wd_ref, *rest, final_norm):
    if final_norm:
        fw_ref, o_ref, xn_ref, h_ref = rest
    else:
        o_ref, xn_ref, h_ref = rest
    x = x_ref[...]
    xn_ref[...] = _rms(x, nw_ref[...]).astype(BF16)
    d_ff = wg_ref.shape[1]
    for c in range(d_ff // FFN_COL_TILE):
        cols = slice(c * FFN_COL_TILE, (c + 1) * FFN_COL_TILE)
        g = _dot(xn_ref[...], wg_ref[:, cols])
        u = _dot(xn_ref[...], wu_ref[:, cols])
        h_ref[:, cols] = (_silu(g) * u).astype(BF16)
    y = x + 0.5 * _dot(h_ref[...], wd_ref[...])
    if final_norm:
        y = _rms(y, fw_ref[...])
    o_ref[...] = y


def _ffn(x, nw, wg, wu, wd, final_w=None):
    m, d = x.shape
    d_ff = wg.shape[1]
    row = pl.BlockSpec((ROW_TILE, d), lambda i: (i, 0))
    in_specs = [row, _resident((1, d)), _resident((d, d_ff)), _resident((d, d_ff)), _resident((d_ff, d))]
    args = [x, nw.reshape(1, d), wg.astype(BF16), wu.astype(BF16), wd.astype(BF16)]
    if final_w is not None:
        in_specs.append(_resident((1, d)))
        args.append(final_w.reshape(1, d))
    return pl.pallas_call(
        functools.partial(_ffn_kernel, final_norm=final_w is not None),
        grid=(m // ROW_TILE,),
        in_specs=in_specs,
        out_specs=row,
        out_shape=jax.ShapeDtypeStruct((m, d), F32),
        scratch_shapes=[pltpu.VMEM((ROW_TILE, d), BF16), pltpu.VMEM((ROW_TILE, d_ff), BF16)],
        compiler_params=_params("parallel"),
        name="ffn",
    )(*args)


Q_A = ATTN_HEADS * HEAD_DIM
KV_A = ATTN_KV_HEADS * HEAD_DIM
QK_B = DN_HEADS * DN_DK
V_B = DN_HEADS * DN_DV
QKV_B = 2 * QK_B + V_B
MAIN_COLS = Q_A + 2 * KV_A + QKV_B + V_B


def _inproj_kernel(x_ref, nw_ref, w_ref, wgate_ref, wgate_t_ref,
                   qa_ref, kva_ref, qkvb_ref, z_ref, gates_ref, gates_t_ref):
    xn = _rms(x_ref[...], nw_ref[...]).astype(BF16)
    c0, c1, c2 = Q_A, Q_A + 2 * KV_A, Q_A + 2 * KV_A + QKV_B
    qa_ref[...] = _dot(xn, w_ref[:, :c0]).astype(BF16)
    kva_ref[...] = _dot(xn, w_ref[:, c0:c1]).astype(BF16)
    qkvb_ref[...] = _dot(xn, w_ref[:, c1:c2])
    z_ref[...] = _dot(xn, w_ref[:, c2:])
    gates_ref[...] = _dot(xn, wgate_ref[...])
    gt = lax.dot_general(wgate_t_ref[...], xn, (((1,), (1,)), ((), ())), preferred_element_type=F32)
    for j in range(ROW_TILE // CHUNK):
        gates_t_ref[j] = gt[:, j * CHUNK:(j + 1) * CHUNK]


def _inproj(x, nw, w_in):
    m, d = x.shape
    w_main = w_in[:, :MAIN_COLS].astype(BF16)
    w_gate = w_in[:, MAIN_COLS:]
    n_gate = w_gate.shape[1]
    w_gate_pad = jnp.pad(w_gate, ((0, 0), (0, GATE_COLS - n_gate))).astype(BF16)
    w_gate_t = w_gate.T.astype(BF16)
    row = lambda n: pl.BlockSpec((ROW_TILE, n), lambda i: (i, 0))
    cpt = ROW_TILE // CHUNK
    return pl.pallas_call(
        _inproj_kernel,
        grid=(m // ROW_TILE,),
        in_specs=[row(d), _resident((1, d)), _resident((d, MAIN_COLS)), _resident((d, GATE_COLS)),
                  _resident((n_gate, d))],
        out_specs=[row(Q_A), row(2 * KV_A), row(QKV_B), row(V_B), row(GATE_COLS),
                   pl.BlockSpec((cpt, n_gate, CHUNK), lambda i: (i, 0, 0))],
        out_shape=[jax.ShapeDtypeStruct((m, Q_A), BF16),
                   jax.ShapeDtypeStruct((m, 2 * KV_A), BF16),
                   jax.ShapeDtypeStruct((m, QKV_B), F32),
                   jax.ShapeDtypeStruct((m, V_B), F32),
                   jax.ShapeDtypeStruct((m, GATE_COLS), F32),
                   jax.ShapeDtypeStruct((m // CHUNK, n_gate, CHUNK), F32)],
        compiler_params=_params("parallel"),
        name="mixer_inproj",
    )(x, nw.reshape(1, d), w_main, w_gate_pad, w_gate_t)


def _alibi_slopes(n_heads):
    return [float(2.0 ** (-8.0 * (h + 1) / n_heads)) for h in range(n_heads)]


def _attn_kernel(sinks_ref, q_ref, kv_ref, kv_prev_ref, o_ref):
    n = pl.program_id(1)
    kv = jnp.concatenate([kv_prev_ref[...], kv_ref[...]], axis=0)
    i = lax.broadcasted_iota(jnp.int32, (ATTN_BLOCK, 2 * ATTN_BLOCK), 0)
    j = lax.broadcasted_iota(jnp.int32, (ATTN_BLOCK, 2 * ATTN_BLOCK), 1)
    dist = i + ATTN_BLOCK - j
    in_window = jnp.where(dist >= 0, jnp.where(dist < WINDOW, 1, 0), 0)
    has_prev = jnp.where(j >= ATTN_BLOCK, 1, jnp.where(n > 0, 1, 0))
    valid = (in_window * has_prev) > 0
    distf = dist.astype(F32)
    slopes = _alibi_slopes(ATTN_HEADS)
    group = ATTN_HEADS // ATTN_KV_HEADS
    outs = []
    for kh in range(ATTN_KV_HEADS):
        k = kv[:, kh * HEAD_DIM:(kh + 1) * HEAD_DIM]
        v = kv[:, KV_A + kh * HEAD_DIM:KV_A + (kh + 1) * HEAD_DIM]
        for g in range(group):
            h = kh * group + g
            q = q_ref[:, h * HEAD_DIM:(h + 1) * HEAD_DIM]
            s = lax.dot_general(q, k, (((1,), (1,)), ((), ())), preferred_element_type=F32)
            s = s * (HEAD_DIM ** -0.5) - slopes[h] * distf
            s = jnp.where(valid, s, -1e30)
            sink = sinks_ref[h]
            mx = jnp.maximum(jnp.max(s, axis=-1, keepdims=True), sink)
            e = jnp.exp(s - mx)
            p = e / (jnp.sum(e, axis=-1, keepdims=True) + jnp.exp(sink - mx))
            outs.append(_dot(p.astype(BF16), v))
    o_ref[...] = jnp.concatenate(outs, axis=1).astype(BF16)


def _attention(qa, kva, sinks, batch, seq):
    nblk = seq // ATTN_BLOCK
    blk = lambda n, fn: pl.BlockSpec((ATTN_BLOCK, n), fn)
    return pl.pallas_call(
        _attn_kernel,
        grid=(batch, nblk),
        in_specs=[pl.BlockSpec(memory_space=pltpu.SMEM),
                  blk(Q_A, lambda b, n: (b * nblk + n, 0)),
                  blk(2 * KV_A, lambda b, n: (b * nblk + n, 0)),
                  blk(2 * KV_A, lambda b, n: (b * nblk + jnp.maximum(n - 1, 0), 0))],
        out_specs=blk(Q_A, lambda b, n: (b * nblk + n, 0)),
        out_shape=jax.ShapeDtypeStruct(qa.shape, BF16),
        compiler_params=_params("parallel", "parallel"),
        name="swa_attention",
    )(sinks.astype(F32), qa, kva, kva)


def _heads(x, base):
    return jnp.stack([x[:, base + h * DN_DK:base + (h + 1) * DN_DK] for h in range(DN_HEADS)], axis=0)


def _l2norm(x):
    return x * lax.rsqrt(jnp.sum(x * x, axis=-1, keepdims=True) + EPS)


def _bmm(a, b, precision=None):
    if precision is None:
        a, b = a.astype(BF16), b.astype(BF16)
    return lax.dot_general(a, b, (((2,), (1,)), ((0,), (0,))), precision=precision,
                           preferred_element_type=F32)


def _bmm_nt(a, b):
    return lax.dot_general(a.astype(BF16), b.astype(BF16), (((2,), (2,)), ((0,), (0,))),
                           preferred_element_type=F32)


def _unit_lower_inverse(low):
    c = low.shape[-1]
    eye = (lax.broadcasted_iota(jnp.int32, (c, c), 0) == lax.broadcasted_iota(jnp.int32, (c, c), 1))
    inv = jnp.where(eye, 1.0, 0.0)[None] - low
    power = low
    span = 2
    while span < c + 1:
        power = _bmm(power, power, HIGHEST)
        inv = inv + _bmm(inv, power, HIGHEST)
        span *= 2
    return inv


def _deltanet_kernel(qkv_ref, halo_ref, z_ref, gates_ref, gates_t_ref, convw_ref,
                     alog_row_ref, dt_row_ref, alog_col_ref, dt_col_ref, normw_ref,
                     o_ref, state_ref, xpad_ref, xc_ref):
    t = pl.program_id(1)
    tt = qkv_ref.shape[0]

    @pl.when(t == 0)
    def _():
        state_ref[...] = jnp.zeros_like(state_ref)
        xpad_ref[0:V7X_SUBLANES, :] = jnp.zeros((V7X_SUBLANES, QKV_B), F32)

    @pl.when(t > 0)
    def _():
        xpad_ref[0:V7X_SUBLANES, :] = halo_ref[...]

    xpad_ref[V7X_SUBLANES:, :] = qkv_ref[...]
    first = V7X_SUBLANES - (DN_CONV - 1)
    acc = convw_ref[0:1, :] * xpad_ref[pl.ds(first, tt), :]
    for k in range(1, DN_CONV):
        acc = acc + convw_ref[k:k + 1, :] * xpad_ref[pl.ds(first + k, tt), :]
    xc_ref[...] = _silu(acc)

    ri = lax.broadcasted_iota(jnp.int32, (CHUNK, CHUNK), 0)
    ci = lax.broadcasted_iota(jnp.int32, (CHUNK, CHUNK), 1)
    causal = ri >= ci
    strict = ri > ci
    tril = jnp.where(causal, 1.0, 0.0)
    triu = jnp.where(ri <= ci, 1.0, 0.0)
    neg_a_row = -jnp.exp(alog_row_ref[...])
    neg_a_col = -jnp.exp(alog_col_ref[...])

    def chunk_body(c, carry):
        r0 = pl.multiple_of(c * CHUNK, CHUNK)
        xc = xc_ref[pl.ds(r0, CHUNK), :]
        q = _l2norm(_heads(xc, 0)) * (DN_DK ** -0.5)
        k = _l2norm(_heads(xc, QK_B))
        v = _heads(xc, 2 * QK_B)

        gates = gates_ref[pl.ds(r0, CHUNK), :]
        beta_all = 1.0 / (1.0 + jnp.exp(-gates))
        g_all = neg_a_row * _softplus(gates + dt_row_ref[...])
        gc_all = jnp.dot(tril, g_all, precision=HIGHEST, preferred_element_type=F32)
        gt = gates_t_ref[c]
        g_t = neg_a_col * _softplus(gt + dt_col_ref[...])
        gc_t = jnp.dot(g_t, triu, precision=HIGHEST, preferred_element_type=F32)

        beta = jnp.stack([beta_all[:, h:h + 1] for h in range(DN_HEADS)], axis=0)
        gc = jnp.stack([gc_all[:, DN_HEADS + h:DN_HEADS + h + 1] for h in range(DN_HEADS)], axis=0)
        diff = jnp.stack([gc_all[:, DN_HEADS + h:DN_HEADS + h + 1] - gc_t[DN_HEADS + h:DN_HEADS + h + 1, :]
                          for h in range(DN_HEADS)], axis=0)
        decay = jnp.where(causal[None], jnp.exp(jnp.where(causal[None], diff, 0.0)), 0.0)
        egc = jnp.exp(gc)
        g_last = gc[:, CHUNK - 1:CHUNK, :]

        kb = k * beta
        low = jnp.where(strict[None], _bmm_nt(kb, k) * decay, 0.0)
        inv = _unit_lower_inverse(low)
        u = _bmm(inv, v * beta, HIGHEST)
        w = _bmm(inv, kb * egc, HIGHEST)
        attn = _bmm_nt(q, k) * decay
        q_dec = q * egc
        k_dec = k * jnp.exp(g_last - gc)

        s = state_ref[...]
        v_new = u - _bmm(w, s)
        o = _bmm(q_dec, s) + _bmm(attn, v_new)
        state_ref[...] = s * jnp.exp(g_last) + _bmm(jnp.swapaxes(k_dec, 1, 2), v_new)

        o = o * lax.rsqrt(jnp.mean(o * o, axis=-1, keepdims=True) + EPS)
        o_cat = jnp.concatenate([o[h] for h in range(DN_HEADS)], axis=1)
        zc = z_ref[pl.ds(r0, CHUNK), :]
        o_ref[pl.ds(r0, CHUNK), :] = (o_cat * normw_ref[...] * _silu(zc)).astype(BF16)
        return carry

    lax.fori_loop(0, tt // CHUNK, chunk_body, 0)


def _deltanet(qkvb, z, gates, gates_t, conv_w, a_log, dt_bias, norm_w, batch, seq):
    m = qkvb.shape[0]
    nt = seq // SEQ_TILE
    n_gate = 2 * DN_HEADS
    row_of = lambda b, t: (b * nt + t, 0)
    halo_rows = SEQ_TILE // V7X_SUBLANES
    pad_row = lambda p: jnp.pad(p.astype(F32), (DN_HEADS, GATE_COLS - 2 * DN_HEADS)).reshape(1, GATE_COLS)
    pad_col = lambda p: jnp.pad(p.astype(F32), (DN_HEADS, 0)).reshape(n_gate, 1)
    return pl.pallas_call(
        _deltanet_kernel,
        grid=(batch, nt),
        in_specs=[pl.BlockSpec((SEQ_TILE, QKV_B), row_of),
                  pl.BlockSpec((V7X_SUBLANES, QKV_B),
                               lambda b, t: (jnp.maximum((b * nt + t) * halo_rows - 1, 0), 0)),
                  pl.BlockSpec((SEQ_TILE, V_B), row_of),
                  pl.BlockSpec((SEQ_TILE, GATE_COLS), row_of),
                  pl.BlockSpec((SEQ_TILE // CHUNK, n_gate, CHUNK), lambda b, t: (b * nt + t, 0, 0)),
                  _resident((DN_CONV, QKV_B)),
                  _resident((1, GATE_COLS)), _resident((1, GATE_COLS)),
                  _resident((n_gate, 1)), _resident((n_gate, 1)),
                  _resident((1, V_B))],
        out_specs=pl.BlockSpec((SEQ_TILE, V_B), row_of),
        out_shape=jax.ShapeDtypeStruct((m, V_B), BF16),
        scratch_shapes=[pltpu.VMEM((DN_HEADS, DN_DK, DN_DV), F32),
                        pltpu.VMEM((SEQ_TILE + V7X_SUBLANES, QKV_B), F32),
                        pltpu.VMEM((SEQ_TILE, QKV_B), F32)],
        compiler_params=_params("parallel", "arbitrary"),
        name="gated_deltanet",
    )(qkvb, qkvb, z, gates, gates_t, conv_w.astype(F32),
      pad_row(a_log), pad_row(dt_bias), pad_col(a_log), pad_col(dt_bias),
      jnp.tile(norm_w.astype(F32), DN_HEADS).reshape(1, V_B))


def _outproj_kernel(x_ref, att_ref, dn_ref, w_ref, o_ref):
    o_ref[...] = x_ref[...] + _dot(att_ref[...], w_ref[:Q_A, :]) + _dot(dn_ref[...], w_ref[Q_A:, :])


def _outproj(x, att, dn, w_out):
    m, d = x.shape
    row = lambda n: pl.BlockSpec((ROW_TILE, n), lambda i: (i, 0))
    return pl.pallas_call(
        _outproj_kernel,
        grid=(m // ROW_TILE,),
        in_specs=[row(d), row(Q_A), row(V_B), _resident(w_out.shape)],
        out_specs=row(d),
        out_shape=jax.ShapeDtypeStruct((m, d), F32),
        compiler_params=_params("parallel"),
        name="mixer_outproj",
    )(x, att, dn, w_out.astype(BF16))


def _glu_kernel(x_ref, nw_ref, w_ref, b_ref, o_ref):
    xn = _rms(x_ref[...], nw_ref[...]).astype(BF16)
    c = o_ref.shape[1]
    a = _dot(xn, w_ref[:, :c]) + b_ref[:, :c]
    g = _dot(xn, w_ref[:, c:]) + b_ref[:, c:]
    o_ref[...] = a * (1.0 / (1.0 + jnp.exp(-g)))


def _glu(x, nw, w, b):
    m, d = x.shape
    c = w.shape[1] // 2
    row = lambda n: pl.BlockSpec((ROW_TILE, n), lambda i: (i, 0))
    return pl.pallas_call(
        _glu_kernel,
        grid=(m // ROW_TILE,),
        in_specs=[row(d), _resident((1, d)), _resident(w.shape), _resident((1, 2 * c))],
        out_specs=row(c),
        out_shape=jax.ShapeDtypeStruct((m, c), F32),
        compiler_params=_params("parallel"),
        name="conformer_glu",
    )(x, nw.reshape(1, d), w.astype(BF16), b.reshape(1, 2 * c))


CONV_HALO = 32


def _dwconv_kernel(x_ref, u_ref, halo_ref, wdw_ref, bdw_ref, lnw_ref, lnb_ref, w2_ref, b2_ref,
                   o_ref, upad_ref, *, width):
    t = pl.program_id(1)
    tt = u_ref.shape[0]

    @pl.when(t == 0)
    def _():
        upad_ref[0:CONV_HALO, :] = jnp.zeros((CONV_HALO, upad_ref.shape[1]), F32)

    @pl.when(t > 0)
    def _():
        upad_ref[0:CONV_HALO, :] = halo_ref[...]

    upad_ref[CONV_HALO:, :] = u_ref[...]
    first = CONV_HALO - (width - 1)
    acc = wdw_ref[0:1, :] * upad_ref[pl.ds(first, tt), :]
    for k in range(1, width):
        acc = acc + wdw_ref[k:k + 1, :] * upad_ref[pl.ds(first + k, tt), :]
    acc = acc + bdw_ref[...]
    mu = jnp.mean(acc, axis=-1, keepdims=True)
    xc = acc - mu
    y = xc * lax.rsqrt(jnp.mean(xc * xc, axis=-1, keepdims=True) + EPS) * lnw_ref[...] + lnb_ref[...]
    y = _silu(y).astype(BF16)
    o_ref[...] = x_ref[...] + _dot(y, w2_ref[...]) + b2_ref[...]


def _dwconv(x, u, w_dw, b_dw, ln_w, ln_b, w2, b2, batch, seq):
    m, d = x.shape
    c = u.shape[1]
    width = w_dw.shape[0]
    nt = seq // SEQ_TILE
    halo_blocks = SEQ_TILE // CONV_HALO
    row_of = lambda b, t: (b * nt + t, 0)
    vec = lambda p: p.astype(F32).reshape(1, -1)
    return pl.pallas_call(
        functools.partial(_dwconv_kernel, width=width),
        grid=(batch, nt),
        in_specs=[pl.BlockSpec((SEQ_TILE, d), row_of),
                  pl.BlockSpec((SEQ_TILE, c), row_of),
                  pl.BlockSpec((CONV_HALO, c),
                               lambda b, t: (jnp.maximum((b * nt + t) * halo_blocks - 1, 0), 0)),
                  _resident((width, c)), _resident((1, c)), _resident((1, c)), _resident((1, c)),
                  _resident((c, d)), _resident((1, d))],
        out_specs=pl.BlockSpec((SEQ_TILE, d), row_of),
        out_shape=jax.ShapeDtypeStruct((m, d), F32),
        scratch_shapes=[pltpu.VMEM((SEQ_TILE + CONV_HALO, c), F32)],
        compiler_params=_params("parallel", "parallel"),
        name="conformer_dwconv",
    )(x, u, u, w_dw.astype(F32), vec(b_dw), vec(ln_w), vec(ln_b), w2.astype(BF16), vec(b2))


def kernel(x, norm_w, ffn_w_gate, ffn_w_up, ffn_w_down, mix_w_in, dn_conv_w, attn_sinks, dn_a_log, dn_dt_bias, dn_norm_w, mix_w_out, conv_w_pw1, conv_b_pw1, conv_w_dw, conv_b_dw, conv_ln_w, conv_ln_b, conv_w_pw2, conv_b_pw2, final_norm_w):
    batch, seq, d = x.shape
    depth = norm_w.shape[0]
    assert seq % SEQ_TILE == 0 and seq % ATTN_BLOCK == 0 and (batch * seq) % ROW_TILE == 0
    assert conv_w_dw.shape[1] <= CONV_HALO + 1 and WINDOW <= ATTN_BLOCK
    assert mix_w_in.shape[2] == MAIN_COLS + 2 * DN_HEADS
    h = x.reshape(batch * seq, d)
    for layer in range(depth):
        h = _ffn(h, norm_w[layer, 0], ffn_w_gate[layer, 0], ffn_w_up[layer, 0], ffn_w_down[layer, 0])
        if layer % 2 == 0:
            e = layer // 2
            qa, kva, qkvb, z, gates, gates_t = _inproj(h, norm_w[layer, 1], mix_w_in[e])
            att = _attention(qa, kva, attn_sinks[e], batch, seq)
            dn = _deltanet(qkvb, z, gates, gates_t, dn_conv_w[e], dn_a_log[e], dn_dt_bias[e],
                           dn_norm_w[e], batch, seq)
            h = _outproj(h, att, dn, mix_w_out[e])
        else:
            c = layer // 2
            u = _glu(h, norm_w[layer, 1], conv_w_pw1[c], conv_b_pw1[c])
            h = _dwconv(h, u, conv_w_dw[c], conv_b_dw[c], conv_ln_w[c], conv_ln_b[c],
                        conv_w_pw2[c], conv_b_pw2[c], batch, seq)
        last = layer == depth - 1
        h = _ffn(h, norm_w[layer, 2], ffn_w_gate[layer, 1], ffn_w_up[layer, 1], ffn_w_down[layer, 1],
                 final_w=final_norm_w if last else None)
    return h.reshape(batch, seq, d)
```

```python
import functools

import jax
import jax.numpy as jnp
from jax import lax
from jax.experimental import pallas as pl
from jax.experimental.pallas import tpu as pltpu

F32 = jnp.float32
BF16 = jnp.bfloat16

ATTN_HEADS = 8
ATTN_KV_HEADS = 2
HEAD_DIM = 64
WINDOW = 128
ATTN_BLOCK = 128
DN_HEADS = 8
DN_DK = 64
DN_DV = 64
DN_CONV = 4
CHUNK = 64
EPS = 1e-6

V7X_LANES = 128
V7X_SUBLANES = 8
V7X_VMEM_LIMIT_BYTES = 56 * 1024 * 1024

ROW_TILE = 512
FFN_COL_TILE = 256
SEQ_TILE = 256
GATE_COLS = V7X_LANES


def _params(*sem):
    return pltpu.CompilerParams(dimension_semantics=sem, vmem_limit_bytes=V7X_VMEM_LIMIT_BYTES)


def _rms(x, w):
    return x * lax.rsqrt(jnp.mean(x * x, axis=-1, keepdims=True) + EPS) * w


def _silu(x):
    return x * (1.0 / (1.0 + jnp.exp(-x)))


def _softplus(x):
    return jnp.maximum(x, 0.0) + jnp.log(1.0 + jnp.exp(-jnp.abs(x)))


def _dot(a, b):
    return jnp.dot(a, b, preferred_element_type=F32)


def _resident(shape):
    return pl.BlockSpec(shape, lambda *_: (0,) * len(shape))


def _ffn_kernel(x_ref, nw_ref, wg_ref, wu_ref, wd_ref, *rest, final_norm):
    if final_norm:
        fw_ref, o_ref, xn_ref, h_ref = rest
    else:
        o_ref, xn_ref, h_ref = rest
    x = x_ref[...]
    xn_ref[...] = _rms(x, nw_ref[...]).astype(BF16)
    d_ff = wg_ref.shape[1]
    for c in range(d_ff // FFN_COL_TILE):
        cols = slice(c * FFN_COL_TILE, (c + 1) * FFN_COL_TILE)
        g = _dot(xn_ref[...], wg_ref[:, cols])
        u = _dot(xn_ref[...], wu_ref[:, cols])
        h_ref[:, cols] = (_silu(g) * u).astype(BF16)
    y = x + 0.5 * _dot(h_ref[...], wd_ref[...])
    if final_norm:
        y = _rms(y, fw_ref[...])
    o_ref[...] = y


def _ffn(x, nw, wg, wu, wd, final_w=None):
    m, d = x.shape
    d_ff = wg.shape[1]
    row = pl.BlockSpec((ROW_TILE, d), lambda i: (i, 0))
    in_specs = [row, _resident((1, d)), _resident((d, d_ff)), _resident((d, d_ff)), _resident((d_ff, d))]
    args = [x, nw.reshape(1, d), wg.astype(BF16), wu.astype(BF16), wd.astype(BF16)]
    if final_w is not None:
        in_specs.append(_resident((1, d)))
        args.append(final_w.reshape(1, d))
    return pl.pallas_call(
        functools.partial(_ffn_kernel, final_norm=final_w is not None),
        grid=(m // ROW_TILE,),
        in_specs=in_specs,
        out_specs=row,
        out_shape=jax.ShapeDtypeStruct((m, d), F32),
        scratch_shapes=[pltpu.VMEM((ROW_TILE, d), BF16), pltpu.VMEM((ROW_TILE, d_ff), BF16)],
        compiler_params=_params("parallel"),
        name="ffn",
    )(*args)


Q_A = ATTN_HEADS * HEAD_DIM
KV_A = ATTN_KV_HEADS * HEAD_DIM
QK_B = DN_HEADS * DN_DK
V_B = DN_HEADS * DN_DV
QKV_B = 2 * QK_B + V_B
MAIN_COLS = Q_A + 2 * KV_A + QKV_B + V_B


def _inproj_kernel(x_ref, nw_ref, w_ref, wgate_ref, qa_ref, kva_ref, qkvb_ref, z_ref, gates_ref):
    xn = _rms(x_ref[...], nw_ref[...]).astype(BF16)
    c0, c1, c2 = Q_A, Q_A + 2 * KV_A, Q_A + 2 * KV_A + QKV_B
    qa_ref[...] = _dot(xn, w_ref[:, :c0]).astype(BF16)
    kva_ref[...] = _dot(xn, w_ref[:, c0:c1]).astype(BF16)
    qkvb_ref[...] = _dot(xn, w_ref[:, c1:c2])
    z_ref[...] = _dot(xn, w_ref[:, c2:])
    gates_ref[...] = _dot(xn, wgate_ref[...])


def _inproj(x, nw, w_in):
    m, d = x.shape
    w_main = w_in[:, :MAIN_COLS].astype(BF16)
    w_gate = w_in[:, MAIN_COLS:]
    w_gate_pad = jnp.pad(w_gate, ((0, 0), (0, GATE_COLS - w_gate.shape[1]))).astype(BF16)
    row = lambda n: pl.BlockSpec((ROW_TILE, n), lambda i: (i, 0))
    return pl.pallas_call(
        _inproj_kernel,
        grid=(m // ROW_TILE,),
        in_specs=[row(d), _resident((1, d)), _resident((d, MAIN_COLS)), _resident((d, GATE_COLS))],
        out_specs=[row(Q_A), row(2 * KV_A), row(QKV_B), row(V_B), row(GATE_COLS)],
        out_shape=[jax.ShapeDtypeStruct((m, Q_A), BF16),
                   jax.ShapeDtypeStruct((m, 2 * KV_A), BF16),
                   jax.ShapeDtypeStruct((m, QKV_B), F32),
                   jax.ShapeDtypeStruct((m, V_B), F32),
                   jax.ShapeDtypeStruct((m, GATE_COLS), F32)],
        compiler_params=_params("parallel"),
        name="mixer_inproj",
    )(x, nw.reshape(1, d), w_main, w_gate_pad)


def _alibi_slopes(n_heads):
    return [float(2.0 ** (-8.0 * (h + 1) / n_heads)) for h in range(n_heads)]


def _attn_kernel(sinks_ref, q_ref, kv_ref, kv_prev_ref, o_ref):
    n = pl.program_id(1)
    kv = jnp.concatenate([kv_prev_ref[...], kv_ref[...]], axis=0)
    i = lax.broadcasted_iota(jnp.int32, (ATTN_BLOCK, 2 * ATTN_BLOCK), 0)
    j = lax.broadcasted_iota(jnp.int32, (ATTN_BLOCK, 2 * ATTN_BLOCK), 1)
    dist = i + ATTN_BLOCK - j
    in_window = jnp.where(dist >= 0, jnp.where(dist < WINDOW, 1, 0), 0)
    has_prev = jnp.where(j >= ATTN_BLOCK, 1, jnp.where(n > 0, 1, 0))
    valid = (in_window * has_prev) > 0
    distf = dist.astype(F32)
    slopes = _alibi_slopes(ATTN_HEADS)
    group = ATTN_HEADS // ATTN_KV_HEADS
    outs = []
    for kh in range(ATTN_KV_HEADS):
        k = kv[:, kh * HEAD_DIM:(kh + 1) * HEAD_DIM]
        v = kv[:, KV_A + kh * HEAD_DIM:KV_A + (kh + 1) * HEAD_DIM]
        for g in range(group):
            h = kh * group + g
            q = q_ref[:, h * HEAD_DIM:(h + 1) * HEAD_DIM]
            s = lax.dot_general(q, k, (((1,), (1,)), ((), ())), preferred_element_type=F32)
            s = s * (HEAD_DIM ** -0.5) - slopes[h] * distf
            s = jnp.where(valid, s, -1e30)
            sink = sinks_ref[h]
            mx = jnp.maximum(jnp.max(s, axis=-1, keepdims=True), sink)
            e = jnp.exp(s - mx)
            p = e / (jnp.sum(e, axis=-1, keepdims=True) + jnp.exp(sink - mx))
            outs.append(_dot(p.astype(BF16), v))
    o_ref[...] = jnp.concatenate(outs, axis=1).astype(BF16)


def _attention(qa, kva, sinks, batch, seq):
    nblk = seq // ATTN_BLOCK
    blk = lambda n, fn: pl.BlockSpec((ATTN_BLOCK, n), fn)
    return pl.pallas_call(
        _attn_kernel,
        grid=(batch, nblk),
        in_specs=[pl.BlockSpec(memory_space=pltpu.SMEM),
                  blk(Q_A, lambda b, n: (b * nblk + n, 0)),
                  blk(2 * KV_A, lambda b, n: (b * nblk + n, 0)),
                  blk(2 * KV_A, lambda b, n: (b * nblk + jnp.maximum(n - 1, 0), 0))],
        out_specs=blk(Q_A, lambda b, n: (b * nblk + n, 0)),
        out_shape=jax.ShapeDtypeStruct(qa.shape, BF16),
        compiler_params=_params("parallel", "parallel"),
        name="swa_attention",
    )(sinks.astype(F32), qa, kva, kva)


PAIR = V7X_LANES // DN_DK
N_PAIRS = DN_HEADS // PAIR


def _split2(x):
    hi = x.astype(BF16)
    return hi, (x - hi.astype(F32)).astype(BF16)


def _split3(x):
    hi = x.astype(BF16)
    rest = x - hi.astype(F32)
    mid = rest.astype(BF16)
    return hi, mid, (rest - mid.astype(F32)).astype(BF16)


def _select_sum(x, onehot, splitter=_split3):
    return sum(_dot(part, onehot) for part in splitter(x))


def _group_sums(x, ones_bd):
    hi, lo = _split2(x)
    cols = []
    for j in range(x.shape[1] // V7X_LANES):
        sl = slice(j * V7X_LANES, (j + 1) * V7X_LANES)
        cols.append(_dot(hi[:, sl], ones_bd) + _dot(lo[:, sl], ones_bd))
    return jnp.concatenate(cols, axis=1)


def _bdot(a, b):
    return lax.dot_general(a, b, (((2,), (1,)), ((0,), (0,))), preferred_element_type=F32)


def _bdot_nt(a, b):
    return lax.dot_general(a, b, (((2,), (2,)), ((0,), (0,))), preferred_element_type=F32)


def _blockdiag(x16, bd16):
    return jnp.concatenate([x16, x16], axis=1) * bd16


def _neumann_lower(low, bd16):
    m = -low
    m16 = m.astype(BF16)
    n = m
    m = _bdot(m16, _blockdiag(m16, bd16))
    span = 2
    while span < CHUNK:
        m16 = m.astype(BF16)
        if 2 * span >= CHUNK:
            n = n + m + _bdot(n.astype(BF16), _blockdiag(m16, bd16))
        else:
            p = _bdot(jnp.concatenate([n.astype(BF16), m16], axis=1), _blockdiag(m16, bd16))
            n = n + m + p[:, :CHUNK]
            m = p[:, CHUNK:]
        span *= 2
    return n


def _deltanet_kernel(qkv_ref, halo_ref, z_ref, gates_ref, convw_ref, alog_ref, dt_ref, normw_ref,
                     o_ref, state_ref, xpad_ref):
    t = pl.program_id(1)
    tt = qkv_ref.shape[0]
    n_chunks = tt // CHUNK

    @pl.when(t == 0)
    def _():
        state_ref[...] = jnp.zeros_like(state_ref)
        xpad_ref[0:V7X_SUBLANES, :] = jnp.zeros((V7X_SUBLANES, QKV_B), F32)

    @pl.when(t > 0)
    def _():
        xpad_ref[0:V7X_SUBLANES, :] = halo_ref[...]

    xpad_ref[V7X_SUBLANES:, :] = qkv_ref[...]
    first = V7X_SUBLANES - (DN_CONV - 1)
    acc = convw_ref[DN_CONV - 1:DN_CONV, :] * xpad_ref[pl.ds(V7X_SUBLANES, tt), :]
    for k in range(DN_CONV - 1):
        acc = acc + convw_ref[k:k + 1, :] * xpad_ref[pl.ds(first + k, tt), :]
    xc = _silu(acc)

    def iota2(shape, axis):
        return lax.broadcasted_iota(jnp.int32, shape, axis)

    sq = (V7X_LANES, V7X_LANES)
    same_head = (iota2(sq, 0) // DN_DK) == (iota2(sq, 1) // DN_DK)
    bd16 = jnp.where(same_head, 1.0, 0.0).astype(BF16)

    qk = xc[:, :2 * QK_B]
    qk = qk * lax.rsqrt(_group_sums(qk * qk, bd16) + EPS)
    q_all = qk[:, :QK_B] * (DN_DK ** -0.5)
    k_all = qk[:, QK_B:]
    v_all = xc[:, 2 * QK_B:]

    gates = gates_ref[...]
    beta_g = 1.0 / (1.0 + jnp.exp(-gates))
    g_g = -jnp.exp(alog_ref[...]) * _softplus(gates + dt_ref[...])
    same_chunk = (iota2((tt, tt), 0) // CHUNK) == (iota2((tt, tt), 1) // CHUNK)
    tril_bd = jnp.where(same_chunk, jnp.where(iota2((tt, tt), 0) >= iota2((tt, tt), 1), 1.0, 0.0),
                        0.0).astype(BF16)
    gc_g = sum(_dot(tril_bd, part) for part in _split3(g_g))
    ex = (GATE_COLS, V_B)
    spread_beta = jnp.where(iota2(ex, 0) == iota2(ex, 1) // DN_DK, 1.0, 0.0).astype(BF16)
    spread_g = jnp.where(iota2(ex, 0) == iota2(ex, 1) // DN_DK + DN_HEADS, 1.0, 0.0).astype(BF16)
    beta_x = _select_sum(beta_g, spread_beta)
    gc_x = _select_sum(gc_g, spread_g)

    egc_x = jnp.exp(gc_x)
    kb_all = k_all * beta_x
    vb_all = v_all * beta_x
    kbe_all = kb_all * egc_x
    qd_all = q_all * egc_x

    units = [(c, p) for c in range(n_chunks) for p in range(N_PAIRS)]
    packed = lambda x: jnp.stack(
        [x[c * CHUNK:(c + 1) * CHUNK, p * V7X_LANES:(p + 1) * V7X_LANES] for c, p in units], axis=0)
    gc = packed(gc_x)
    g_last = gc[:, CHUNK - 1:CHUNK, :]
    k = packed(k_all)
    k_dec_t = jnp.swapaxes(k * jnp.exp(g_last - gc), 1, 2).astype(BF16)
    state_gain = jnp.exp(g_last)

    ii = iota2((CHUNK, V7X_LANES), 0)
    jj = iota2((CHUNK, V7X_LANES), 1) % CHUNK
    causal = (ii >= jj)[None]
    strict = (ii > jj)[None]
    gc_row = jnp.sum(jnp.where((ii == jj)[None], gc, 0.0), axis=1, keepdims=True)
    decay = jnp.where(causal, jnp.exp(jnp.where(causal, gc - gc_row, 0.0)), 0.0)

    k_bd = _blockdiag(k.astype(BF16), bd16)
    kq = jnp.concatenate([packed(kb_all).astype(BF16), packed(q_all).astype(BF16)], axis=1)
    kk = _bdot_nt(kq, k_bd)
    low = jnp.where(strict, kk[:, :CHUNK] * decay, 0.0)
    attn = (kk[:, CHUNK:] * decay).astype(BF16)

    n16 = _neumann_lower(low, bd16).astype(BF16)
    vb = packed(vb_all)
    kbe = packed(kbe_all)
    rhs = jnp.concatenate([_blockdiag(vb.astype(BF16), bd16), _blockdiag(kbe.astype(BF16), bd16)], axis=2)
    uw = _bdot(n16, rhs)
    u = vb + uw[:, :, :V7X_LANES]
    w = kbe + uw[:, :, V7X_LANES:]
    state_lhs = jnp.concatenate([w.astype(BF16), packed(qd_all).astype(BF16)], axis=1)

    low_lanes = iota2((DN_DK, V7X_LANES), 1) < DN_DV
    s = state_ref[...]
    outs = []
    for c in range(n_chunks):
        g = slice(c * N_PAIRS, (c + 1) * N_PAIRS)
        ps = _bdot(state_lhs[g], _blockdiag(s.astype(BF16), bd16))
        v_new = (u[g] - ps[:, :CHUNK]).astype(BF16)
        outs.append(ps[:, CHUNK:] + _bdot(attn[g], _blockdiag(v_new, bd16)))
        upd = _bdot(k_dec_t[g], v_new)
        s = s * state_gain[g] + jnp.where(low_lanes, upd[:, :DN_DK], upd[:, DN_DK:])
    state_ref[...] = s

    o_all = jnp.concatenate(
        [jnp.concatenate([o_c[p] for p in range(N_PAIRS)], axis=1) for o_c in outs], axis=0)
    o_all = o_all * lax.rsqrt(_group_sums(o_all * o_all, bd16) * (1.0 / DN_DV) + EPS)
    o_ref[...] = (o_all * normw_ref[...] * _silu(z_ref[...])).astype(BF16)


def _deltanet(qkvb, z, gates, conv_w, a_log, dt_bias, norm_w, batch, seq):
    m = qkvb.shape[0]
    nt = seq // SEQ_TILE
    row_of = lambda b, t: (b * nt + t, 0)
    halo_blocks = SEQ_TILE // V7X_SUBLANES
    pad_row = lambda p: jnp.pad(p.astype(F32), (DN_HEADS, GATE_COLS - 2 * DN_HEADS)).reshape(1, GATE_COLS)
    return pl.pallas_call(
        _deltanet_kernel,
        grid=(batch, nt),
        in_specs=[pl.BlockSpec((SEQ_TILE, QKV_B), row_of),
                  pl.BlockSpec((V7X_SUBLANES, QKV_B),
                               lambda b, t: (jnp.maximum((b * nt + t) * halo_blocks - 1, 0), 0)),
                  pl.BlockSpec((SEQ_TILE, V_B), row_of),
                  pl.BlockSpec((SEQ_TILE, GATE_COLS), row_of),
                  _resident((DN_CONV, QKV_B)),
                  _resident((1, GATE_COLS)), _resident((1, GATE_COLS)),
                  _resident((1, V_B))],
        out_specs=pl.BlockSpec((SEQ_TILE, V_B), row_of),
        out_shape=jax.ShapeDtypeStruct((m, V_B), BF16),
        scratch_shapes=[pltpu.VMEM((N_PAIRS, DN_DK, V7X_LANES), F32),
                        pltpu.VMEM((SEQ_TILE + V7X_SUBLANES, QKV_B), F32)],
        compiler_params=_params("parallel", "arbitrary"),
        name="gated_deltanet",
    )(qkvb, qkvb, z, gates, conv_w.astype(F32), pad_row(a_log), pad_row(dt_bias),
      jnp.tile(norm_w.astype(F32), DN_HEADS).reshape(1, V_B))


def _outproj_kernel(x_ref, att_ref, dn_ref, w_ref, o_ref):
    o_ref[...] = x_ref[...] + _dot(att_ref[...], w_ref[:Q_A, :]) + _dot(dn_ref[...], w_ref[Q_A:, :])


def _outproj(x, att, dn, w_out):
    m, d = x.shape
    row = lambda n: pl.BlockSpec((ROW_TILE, n), lambda i: (i, 0))
    return pl.pallas_call(
        _outproj_kernel,
        grid=(m // ROW_TILE,),
        in_specs=[row(d), row(Q_A), row(V_B), _resident(w_out.shape)],
        out_specs=row(d),
        out_shape=jax.ShapeDtypeStruct((m, d), F32),
        compiler_params=_params("parallel"),
        name="mixer_outproj",
    )(x, att, dn, w_out.astype(BF16))


def _glu_kernel(x_ref, nw_ref, w_ref, b_ref, o_ref):
    xn = _rms(x_ref[...], nw_ref[...]).astype(BF16)
    c = o_ref.shape[1]
    a = _dot(xn, w_ref[:, :c]) + b_ref[:, :c]
    g = _dot(xn, w_ref[:, c:]) + b_ref[:, c:]
    o_ref[...] = a * (1.0 / (1.0 + jnp.exp(-g)))


def _glu(x, nw, w, b):
    m, d = x.shape
    c = w.shape[1] // 2
    row = lambda n: pl.BlockSpec((ROW_TILE, n), lambda i: (i, 0))
    return pl.pallas_call(
        _glu_kernel,
        grid=(m // ROW_TILE,),
        in_specs=[row(d), _resident((1, d)), _resident(w.shape), _resident((1, 2 * c))],
        out_specs=row(c),
        out_shape=jax.ShapeDtypeStruct((m, c), F32),
        compiler_params=_params("parallel"),
        name="conformer_glu",
    )(x, nw.reshape(1, d), w.astype(BF16), b.reshape(1, 2 * c))


CONV_HALO = 32


def _dwconv_kernel(x_ref, u_ref, halo_ref, wdw_ref, bdw_ref, lnw_ref, lnb_ref, w2_ref, b2_ref,
                   o_ref, upad_ref, shift_ref, *, width):
    t = pl.program_id(1)
    tt = u_ref.shape[0]

    @pl.when(t == 0)
    def _():
        upad_ref[0:CONV_HALO, :] = jnp.zeros((CONV_HALO, upad_ref.shape[1]), F32)

    @pl.when(t > 0)
    def _():
        upad_ref[0:CONV_HALO, :] = halo_ref[...]

    upad_ref[CONV_HALO:, :] = u_ref[...]
    span = tt + CONV_HALO - V7X_SUBLANES
    for s in range(1, V7X_SUBLANES):
        shift_ref[s - 1, 0:span, :] = upad_ref[pl.ds(s, span), :]
    first = CONV_HALO - (width - 1)
    acc = bdw_ref[...]
    for k in range(width):
        tile_row, s = divmod(first + k, V7X_SUBLANES)
        start = tile_row * V7X_SUBLANES
        src = upad_ref[pl.ds(start, tt), :] if s == 0 else shift_ref[s - 1, pl.ds(start, tt), :]
        acc = acc + wdw_ref[k:k + 1, :] * src
    mu = jnp.mean(acc, axis=-1, keepdims=True)
    xc = acc - mu
    y = xc * lax.rsqrt(jnp.mean(xc * xc, axis=-1, keepdims=True) + EPS) * lnw_ref[...] + lnb_ref[...]
    y = _silu(y).astype(BF16)
    o_ref[...] = x_ref[...] + _dot(y, w2_ref[...]) + b2_ref[...]


def _dwconv(x, u, w_dw, b_dw, ln_w, ln_b, w2, b2, batch, seq):
    m, d = x.shape
    c = u.shape[1]
    width = w_dw.shape[0]
    nt = seq // SEQ_TILE
    halo_blocks = SEQ_TILE // CONV_HALO
    row_of = lambda b, t: (b * nt + t, 0)
    vec = lambda p: p.astype(F32).reshape(1, -1)
    return pl.pallas_call(
        functools.partial(_dwconv_kernel, width=width),
        grid=(batch, nt),
        in_specs=[pl.BlockSpec((SEQ_TILE, d), row_of),
                  pl.BlockSpec((SEQ_TILE, c), row_of),
                  pl.BlockSpec((CONV_HALO, c),
                               lambda b, t: (jnp.maximum((b * nt + t) * halo_blocks - 1, 0), 0)),
                  _resident((width, c)), _resident((1, c)), _resident((1, c)), _resident((1, c)),
                  _resident((c, d)), _resident((1, d))],
        out_specs=pl.BlockSpec((SEQ_TILE, d), row_of),
        out_shape=jax.ShapeDtypeStruct((m, d), F32),
        scratch_shapes=[pltpu.VMEM((SEQ_TILE + CONV_HALO, c), F32),
                        pltpu.VMEM((V7X_SUBLANES - 1, SEQ_TILE + CONV_HALO, c), F32)],
        compiler_params=_params("parallel", "parallel"),
        name="conformer_dwconv",
    )(x, u, u, w_dw.astype(F32), vec(b_dw), vec(ln_w), vec(ln_b), w2.astype(BF16), vec(b2))


def kernel(x, norm_w, ffn_w_gate, ffn_w_up, ffn_w_down, mix_w_in, dn_conv_w, attn_sinks, dn_a_log, dn_dt_bias, dn_norm_w, mix_w_out, conv_w_pw1, conv_b_pw1, conv_w_dw, conv_b_dw, conv_ln_w, conv_ln_b, conv_w_pw2, conv_b_pw2, final_norm_w):
    batch, seq, d = x.shape
    depth = norm_w.shape[0]
    assert seq % SEQ_TILE == 0 and seq % ATTN_BLOCK == 0 and (batch * seq) % ROW_TILE == 0
    assert conv_w_dw.shape[1] <= CONV_HALO + 1 and WINDOW <= ATTN_BLOCK
    assert mix_w_in.shape[2] == MAIN_COLS + 2 * DN_HEADS
    h = x.reshape(batch * seq, d)
    for layer in range(depth):
        h = _ffn(h, norm_w[layer, 0], ffn_w_gate[layer, 0], ffn_w_up[layer, 0], ffn_w_down[layer, 0])
        if layer % 2 == 0:
            e = layer // 2
            qa, kva, qkvb, z, gates = _inproj(h, norm_w[layer, 1], mix_w_in[e])
            att = _attention(qa, kva, attn_sinks[e], batch, seq)
            dn = _deltanet(qkvb, z, gates, dn_conv_w[e], dn_a_log[e], dn_dt_bias[e], dn_norm_w[e],
                           batch, seq)
            h = _outproj(h, att, dn, mix_w_out[e])
        else:
            c = layer // 2
            u = _glu(h, norm_w[layer, 1], conv_w_pw1[c], conv_b_pw1[c])
            h = _dwconv(h, u, conv_w_dw[c], conv_b_dw[c], conv_ln_w[c], conv_ln_b[c],
                        conv_w_pw2[c], conv_b_pw2[c], batch, seq)
        last = layer == depth - 1
        h = _ffn(h, norm_w[layer, 2], ffn_w_gate[layer, 1], ffn_w_up[layer, 1], ffn_w_down[layer, 1],
                 final_w=final_norm_w if last else None)
    return h.reshape(batch, seq, d)
```

```python
import functools

import jax
import jax.numpy as jnp
from jax import lax
from jax.experimental import pallas as pl
from jax.experimental.pallas import tpu as pltpu

F32 = jnp.float32
BF16 = jnp.bfloat16

ATTN_HEADS = 8
ATTN_KV_HEADS = 2
HEAD_DIM = 64
WINDOW = 128
ATTN_BLOCK = 128
DN_HEADS = 8
DN_DK = 64
DN_DV = 64
DN_CONV = 4
CHUNK = 64
EPS = 1e-6

V7X_LANES = 128
V7X_SUBLANES = 8
V7X_VMEM_LIMIT_BYTES = 56 * 1024 * 1024

ROW_TILE = 512
FFN_COL_TILE = 256
SEQ_TILE = 256
GATE_COLS = V7X_LANES


def _params(*sem):
    return pltpu.CompilerParams(dimension_semantics=sem, vmem_limit_bytes=V7X_VMEM_LIMIT_BYTES)


def _rms(x, w):
    return x * lax.rsqrt(jnp.mean(x * x, axis=-1, keepdims=True) + EPS) * w


def _silu(x):
    return x * (1.0 / (1.0 + jnp.exp(-x)))


def _softplus(x):
    return jnp.maximum(x, 0.0) + jnp.log(1.0 + jnp.exp(-jnp.abs(x)))


def _dot(a, b):
    return jnp.dot(a, b, preferred_element_type=F32)


def _resident(shape):
    return pl.BlockSpec(shape, lambda *_: (0,) * len(shape))


def _ffn_kernel(x_ref, nw_ref, wg_ref, wu_ref, wd_ref, *rest, final_norm):
    if final_norm:
        fw_ref, o_ref, xn_ref, h_ref = rest
    else:
        o_ref, xn_ref, h_ref = rest
    x = x_ref[...]
    xn_ref[...] = _rms(x, nw_ref[...]).astype(BF16)
    d_ff = wg_ref.shape[1]
    for c in range(d_ff // FFN_COL_TILE):
        cols = slice(c * FFN_COL_TILE, (c + 1) * FFN_COL_TILE)
        g = _dot(xn_ref[...], wg_ref[:, cols])
        u = _dot(xn_ref[...], wu_ref[:, cols])
        h_ref[:, cols] = (_silu(g) * u).astype(BF16)
    y = x + 0.5 * _dot(h_ref[...], wd_ref[...])
    if final_norm:
        y = _rms(y, fw_ref[...])
    o_ref[...] = y


def _ffn(x, nw, wg, wu, wd, final_w=None):
    m, d = x.shape
    d_ff = wg.shape[1]
    row = pl.BlockSpec((ROW_TILE, d), lambda i: (i, 0))
    in_specs = [row, _resident((1, d)), _resident((d, d_ff)), _resident((d, d_ff)), _resident((d_ff, d))]
    args = [x, nw.reshape(1, d), wg.astype(BF16), wu.astype(BF16), wd.astype(BF16)]
    if final_w is not None:
        in_specs.append(_resident((1, d)))
        args.append(final_w.reshape(1, d))
    return pl.pallas_call(
        functools.partial(_ffn_kernel, final_norm=final_w is not None),
        grid=(m // ROW_TILE,),
        in_specs=in_specs,
        out_specs=row,
        out_shape=jax.ShapeDtypeStruct((m, d), F32),
        scratch_shapes=[pltpu.VMEM((ROW_TILE, d), BF16), pltpu.VMEM((ROW_TILE, d_ff), BF16)],
        compiler_params=_params("parallel"),
        name="ffn",
    )(*args)


Q_A = ATTN_HEADS * HEAD_DIM
KV_A = ATTN_KV_HEADS * HEAD_DIM
QK_B = DN_HEADS * DN_DK
V_B = DN_HEADS * DN_DV
QKV_B = 2 * QK_B + V_B
MAIN_COLS = Q_A + 2 * KV_A + QKV_B + V_B


def _inproj_kernel(x_ref, nw_ref, w_ref, wgate_ref, qa_ref, kva_ref, qkvb_ref, z_ref, gates_ref):
    xn = _rms(x_ref[...], nw_ref[...]).astype(BF16)
    c0, c1, c2 = Q_A, Q_A + 2 * KV_A, Q_A + 2 * KV_A + QKV_B
    qa_ref[...] = _dot(xn, w_ref[:, :c0]).astype(BF16)
    kva_ref[...] = _dot(xn, w_ref[:, c0:c1])
    qkvb_ref[...] = _dot(xn, w_ref[:, c1:c2])
    z_ref[...] = _dot(xn, w_ref[:, c2:])
    gates_ref[...] = _dot(xn, wgate_ref[...])


def _inproj(x, nw, w_in):
    m, d = x.shape
    w_main = w_in[:, :MAIN_COLS].astype(BF16)
    w_gate = w_in[:, MAIN_COLS:]
    w_gate_pad = jnp.pad(w_gate, ((0, 0), (0, GATE_COLS - w_gate.shape[1]))).astype(BF16)
    row = lambda n: pl.BlockSpec((ROW_TILE, n), lambda i: (i, 0))
    return pl.pallas_call(
        _inproj_kernel,
        grid=(m // ROW_TILE,),
        in_specs=[row(d), _resident((1, d)), _resident((d, MAIN_COLS)), _resident((d, GATE_COLS))],
        out_specs=[row(Q_A), row(2 * KV_A), row(QKV_B), row(V_B), row(GATE_COLS)],
        out_shape=[jax.ShapeDtypeStruct((m, Q_A), BF16),
                   jax.ShapeDtypeStruct((m, 2 * KV_A), F32),
                   jax.ShapeDtypeStruct((m, QKV_B), F32),
                   jax.ShapeDtypeStruct((m, V_B), F32),
                   jax.ShapeDtypeStruct((m, GATE_COLS), F32)],
        compiler_params=_params("parallel"),
        name="mixer_inproj",
    )(x, nw.reshape(1, d), w_main, w_gate_pad)


def _alibi_slopes(n_heads):
    return [float(2.0 ** (-8.0 * (h + 1) / n_heads)) for h in range(n_heads)]


ATTN_Q_TILE = 512


def _attn_kernel(sinks_ref, q_ref, kv_ref, kv_prev_ref, o_ref):
    t = pl.program_id(1)
    half = lax.broadcasted_iota(jnp.int32, (1, V7X_LANES), 1) < HEAD_DIM
    kv = jnp.concatenate([kv_prev_ref[...], kv_ref[...]], axis=0)

    def placed(tile):
        swapped = pltpu.roll(tile, HEAD_DIM, axis=1)
        keep_lo = lambda x: jnp.where(half, x, 0.0).astype(BF16)
        keep_hi = lambda x: jnp.where(half, 0.0, x).astype(BF16)
        return {(0, 0): keep_lo(tile), (1, 1): keep_hi(tile), (0, 1): keep_hi(swapped), (1, 0): keep_lo(swapped)}

    k_at = placed(kv[:, :KV_A])
    v_at = placed(kv[:, KV_A:])

    i = lax.broadcasted_iota(jnp.int32, (ATTN_BLOCK, 2 * ATTN_BLOCK), 0)
    j = lax.broadcasted_iota(jnp.int32, (ATTN_BLOCK, 2 * ATTN_BLOCK), 1)
    dist = i + ATTN_BLOCK - j
    valid = jnp.where(dist >= 0, jnp.where(dist < WINDOW, 1, 0), 0) > 0
    no_prev = jnp.where(j < ATTN_BLOCK, jnp.where(t == 0, 1, 0), 0) > 0
    distf = dist.astype(F32)
    slopes = _alibi_slopes(ATTN_HEADS)
    group = ATTN_HEADS // ATTN_KV_HEADS
    bias = [jnp.where(valid, -slopes[h] * distf, -1e30) for h in range(ATTN_HEADS)]
    q = q_ref[...] * (HEAD_DIM ** -0.5)

    for qb in range(ATTN_Q_TILE // ATTN_BLOCK):
        rows = slice(qb * ATTN_BLOCK, (qb + 1) * ATTN_BLOCK)
        win = slice(qb * ATTN_BLOCK, (qb + 2) * ATTN_BLOCK)
        tiles = []
        for pair in range(ATTN_HEADS // 2):
            q_pair = q[rows, pair * V7X_LANES:(pair + 1) * V7X_LANES]
            o_pair = None
            for par in range(2):
                h = 2 * pair + par
                kh = h // group
                b_h = jnp.where(no_prev, -1e30, bias[h]) if qb == 0 else bias[h]
                s = lax.dot_general(q_pair, k_at[kh, par][win], (((1,), (1,)), ((), ())),
                                    preferred_element_type=F32) + b_h
                sink = sinks_ref[h]
                mx = jnp.maximum(jnp.max(s, axis=-1, keepdims=True), sink)
                e = jnp.exp(s - mx)
                inv = 1.0 / (jnp.sum(e, axis=-1, keepdims=True) + jnp.exp(sink - mx))
                o_h = _dot(e.astype(BF16), v_at[kh, par][win]) * inv
                o_pair = o_h if o_pair is None else o_pair + o_h
            tiles.append(o_pair)
        o_ref[rows, :] = jnp.concatenate(tiles, axis=1).astype(BF16)


def _attention(qa, kva, sinks, batch, seq):
    assert KV_A == V7X_LANES and ATTN_KV_HEADS == 2 and seq % ATTN_Q_TILE == 0
    nt = seq // ATTN_Q_TILE
    prev_blocks = ATTN_Q_TILE // ATTN_BLOCK
    row_of = lambda b, t: (b * nt + t, 0)
    return pl.pallas_call(
        _attn_kernel,
        grid=(batch, nt),
        in_specs=[pl.BlockSpec(memory_space=pltpu.SMEM),
                  pl.BlockSpec((ATTN_Q_TILE, Q_A), row_of),
                  pl.BlockSpec((ATTN_Q_TILE, 2 * KV_A), row_of),
                  pl.BlockSpec((ATTN_BLOCK, 2 * KV_A),
                               lambda b, t: (jnp.maximum((b * nt + t) * prev_blocks - 1, 0), 0))],
        out_specs=pl.BlockSpec((ATTN_Q_TILE, Q_A), row_of),
        out_shape=jax.ShapeDtypeStruct(qa.shape, BF16),
        compiler_params=_params("parallel", "parallel"),
        name="swa_attention",
    )(sinks.astype(F32), qa, kva, kva)


PAIR = V7X_LANES // DN_DK
N_PAIRS = DN_HEADS // PAIR


def _split2(x):
    hi = x.astype(BF16)
    return hi, (x - hi.astype(F32)).astype(BF16)


def _split3(x):
    hi = x.astype(BF16)
    rest = x - hi.astype(F32)
    mid = rest.astype(BF16)
    return hi, mid, (rest - mid.astype(F32)).astype(BF16)


def _select_sum(x, onehot, splitter=_split3):
    return sum(_dot(part, onehot) for part in splitter(x))


def _group_sums(x, ones_bd):
    hi, lo = _split2(x)
    cols = []
    for j in range(x.shape[1] // V7X_LANES):
        sl = slice(j * V7X_LANES, (j + 1) * V7X_LANES)
        cols.append(_dot(hi[:, sl], ones_bd) + _dot(lo[:, sl], ones_bd))
    return jnp.concatenate(cols, axis=1)


def _bdot(a, b):
    return lax.dot_general(a, b, (((2,), (1,)), ((0,), (0,))), preferred_element_type=F32)


def _bdot_nt(a, b):
    return lax.dot_general(a, b, (((2,), (2,)), ((0,), (0,))), preferred_element_type=F32)


def _blockdiag(x16, bd16):
    return jnp.concatenate([x16, x16], axis=1) * bd16


def _neumann_lower(low, bd16):
    m = -low
    m16 = m.astype(BF16)
    n = m
    m = _bdot(m16, _blockdiag(m16, bd16))
    span = 2
    while span < CHUNK:
        m16 = m.astype(BF16)
        if 2 * span >= CHUNK:
            n = n + m + _bdot(n.astype(BF16), _blockdiag(m16, bd16))
        else:
            p = _bdot(jnp.concatenate([n.astype(BF16), m16], axis=1), _blockdiag(m16, bd16))
            n = n + m + p[:, :CHUNK]
            m = p[:, CHUNK:]
        span *= 2
    return n


def _deltanet_kernel(qkv_ref, halo_ref, z_ref, gates_ref, convw_ref, alog_ref, dt_ref, normw_ref,
                     o_ref, state_ref, xpad_ref):
    t = pl.program_id(1)
    tt = qkv_ref.shape[0]
    n_chunks = tt // CHUNK

    @pl.when(t == 0)
    def _():
        state_ref[...] = jnp.zeros_like(state_ref)
        xpad_ref[0:V7X_SUBLANES, :] = jnp.zeros((V7X_SUBLANES, QKV_B), F32)

    @pl.when(t > 0)
    def _():
        xpad_ref[0:V7X_SUBLANES, :] = halo_ref[...]

    xpad_ref[V7X_SUBLANES:, :] = qkv_ref[...]
    first = V7X_SUBLANES - (DN_CONV - 1)
    acc = convw_ref[DN_CONV - 1:DN_CONV, :] * xpad_ref[pl.ds(V7X_SUBLANES, tt), :]
    for k in range(DN_CONV - 1):
        acc = acc + convw_ref[k:k + 1, :] * xpad_ref[pl.ds(first + k, tt), :]
    xc = _silu(acc)

    def iota2(shape, axis):
        return lax.broadcasted_iota(jnp.int32, shape, axis)

    sq = (V7X_LANES, V7X_LANES)
    same_head = (iota2(sq, 0) // DN_DK) == (iota2(sq, 1) // DN_DK)
    bd16 = jnp.where(same_head, 1.0, 0.0).astype(BF16)

    qk = xc[:, :2 * QK_B]
    qk = qk * lax.rsqrt(_group_sums(qk * qk, bd16) + EPS)
    q_all = qk[:, :QK_B] * (DN_DK ** -0.5)
    k_all = qk[:, QK_B:]
    v_all = xc[:, 2 * QK_B:]

    gates = gates_ref[...]
    beta_g = 1.0 / (1.0 + jnp.exp(-gates))
    g_g = -jnp.exp(alog_ref[...]) * _softplus(gates + dt_ref[...])
    same_chunk = (iota2((tt, tt), 0) // CHUNK) == (iota2((tt, tt), 1) // CHUNK)
    tril_bd = jnp.where(same_chunk, jnp.where(iota2((tt, tt), 0) >= iota2((tt, tt), 1), 1.0, 0.0),
                        0.0).astype(BF16)
    gc_g = sum(_dot(tril_bd, part) for part in _split3(g_g))
    ex = (GATE_COLS, V_B)
    spread_beta = jnp.where(iota2(ex, 0) == iota2(ex, 1) // DN_DK, 1.0, 0.0).astype(BF16)
    spread_g = jnp.where(iota2(ex, 0) == iota2(ex, 1) // DN_DK + DN_HEADS, 1.0, 0.0).astype(BF16)
    beta_x = _select_sum(beta_g, spread_beta)
    gc_x = _select_sum(gc_g, spread_g)

    egc_x = jnp.exp(gc_x)
    kb_all = k_all * beta_x
    vb_all = v_all * beta_x
    kbe_all = kb_all * egc_x
    qd_all = q_all * egc_x

    units = [(c, p) for c in range(n_chunks) for p in range(N_PAIRS)]
    packed = lambda x: jnp.stack(
        [x[c * CHUNK:(c + 1) * CHUNK, p * V7X_LANES:(p + 1) * V7X_LANES] for c, p in units], axis=0)
    gc = packed(gc_x)
    g_last = gc[:, CHUNK - 1:CHUNK, :]
    k = packed(k_all)
    k_dec_t = jnp.swapaxes(k * jnp.exp(g_last - gc), 1, 2).astype(BF16)
    state_gain = jnp.exp(g_last)

    ii = iota2((CHUNK, V7X_LANES), 0)
    jj = iota2((CHUNK, V7X_LANES), 1) % CHUNK
    causal = (ii >= jj)[None]
    strict = (ii > jj)[None]
    gc_row = jnp.sum(jnp.where((ii == jj)[None], gc, 0.0), axis=1, keepdims=True)
    decay = jnp.where(causal, jnp.exp(jnp.where(causal, gc - gc_row, 0.0)), 0.0)

    k_bd = _blockdiag(k.astype(BF16), bd16)
    kq = jnp.concatenate([packed(kb_all).astype(BF16), packed(q_all).astype(BF16)], axis=1)
    kk = _bdot_nt(kq, k_bd)
    low = jnp.where(strict, kk[:, :CHUNK] * decay, 0.0)
    attn = (kk[:, CHUNK:] * decay).astype(BF16)

    n16 = _neumann_lower(low, bd16).astype(BF16)
    vb = packed(vb_all)
    kbe = packed(kbe_all)
    rhs = jnp.concatenate([_blockdiag(vb.astype(BF16), bd16), _blockdiag(kbe.astype(BF16), bd16)], axis=2)
    uw = _bdot(n16, rhs)
    u = vb + uw[:, :, :V7X_LANES]
    w = kbe + uw[:, :, V7X_LANES:]
    state_lhs = jnp.concatenate([w.astype(BF16), packed(qd_all).astype(BF16)], axis=1)

    low_lanes = iota2((DN_DK, V7X_LANES), 1) < DN_DV
    s = state_ref[...]
    outs = []
    for c in range(n_chunks):
        g = slice(c * N_PAIRS, (c + 1) * N_PAIRS)
        ps = _bdot(state_lhs[g], _blockdiag(s.astype(BF16), bd16))
        v_new = (u[g] - ps[:, :CHUNK]).astype(BF16)
        outs.append(ps[:, CHUNK:] + _bdot(attn[g], _blockdiag(v_new, bd16)))
        upd = _bdot(k_dec_t[g], v_new)
        s = s * state_gain[g] + jnp.where(low_lanes, upd[:, :DN_DK], upd[:, DN_DK:])
    state_ref[...] = s

    o_all = jnp.concatenate(
        [jnp.concatenate([o_c[p] for p in range(N_PAIRS)], axis=1) for o_c in outs], axis=0)
    o_all = o_all * lax.rsqrt(_group_sums(o_all * o_all, bd16) * (1.0 / DN_DV) + EPS)
    o_ref[...] = (o_all * normw_ref[...] * _silu(z_ref[...])).astype(BF16)


def _deltanet(qkvb, z, gates, conv_w, a_log, dt_bias, norm_w, batch, seq):
    m = qkvb.shape[0]
    nt = seq // SEQ_TILE
    row_of = lambda b, t: (b * nt + t, 0)
    halo_blocks = SEQ_TILE // V7X_SUBLANES
    pad_row = lambda p: jnp.pad(p.astype(F32), (DN_HEADS, GATE_COLS - 2 * DN_HEADS)).reshape(1, GATE_COLS)
    return pl.pallas_call(
        _deltanet_kernel,
        grid=(batch, nt),
        in_specs=[pl.BlockSpec((SEQ_TILE, QKV_B), row_of),
                  pl.BlockSpec((V7X_SUBLANES, QKV_B),
                               lambda b, t: (jnp.maximum((b * nt + t) * halo_blocks - 1, 0), 0)),
                  pl.BlockSpec((SEQ_TILE, V_B), row_of),
                  pl.BlockSpec((SEQ_TILE, GATE_COLS), row_of),
                  _resident((DN_CONV, QKV_B)),
                  _resident((1, GATE_COLS)), _resident((1, GATE_COLS)),
                  _resident((1, V_B))],
        out_specs=pl.BlockSpec((SEQ_TILE, V_B), row_of),
        out_shape=jax.ShapeDtypeStruct((m, V_B), BF16),
        scratch_shapes=[pltpu.VMEM((N_PAIRS, DN_DK, V7X_LANES), F32),
                        pltpu.VMEM((SEQ_TILE + V7X_SUBLANES, QKV_B), F32)],
        compiler_params=_params("parallel", "arbitrary"),
        name="gated_deltanet",
    )(qkvb, qkvb, z, gates, conv_w.astype(F32), pad_row(a_log), pad_row(dt_bias),
      jnp.tile(norm_w.astype(F32), DN_HEADS).reshape(1, V_B))


def _outproj_kernel(x_ref, att_ref, dn_ref, w_ref, o_ref):
    o_ref[...] = x_ref[...] + _dot(att_ref[...], w_ref[:Q_A, :]) + _dot(dn_ref[...], w_ref[Q_A:, :])


def _outproj(x, att, dn, w_out):
    m, d = x.shape
    row = lambda n: pl.BlockSpec((ROW_TILE, n), lambda i: (i, 0))
    return pl.pallas_call(
        _outproj_kernel,
        grid=(m // ROW_TILE,),
        in_specs=[row(d), row(Q_A), row(V_B), _resident(w_out.shape)],
        out_specs=row(d),
        out_shape=jax.ShapeDtypeStruct((m, d), F32),
        compiler_params=_params("parallel"),
        name="mixer_outproj",
    )(x, att, dn, w_out.astype(BF16))


def _glu_kernel(x_ref, nw_ref, w_ref, b_ref, o_ref):
    xn = _rms(x_ref[...], nw_ref[...]).astype(BF16)
    c = o_ref.shape[1]
    a = _dot(xn, w_ref[:, :c]) + b_ref[:, :c]
    g = _dot(xn, w_ref[:, c:]) + b_ref[:, c:]
    o_ref[...] = a * (1.0 / (1.0 + jnp.exp(-g)))


def _glu(x, nw, w, b):
    m, d = x.shape
    c = w.shape[1] // 2
    row = lambda n: pl.BlockSpec((ROW_TILE, n), lambda i: (i, 0))
    return pl.pallas_call(
        _glu_kernel,
        grid=(m // ROW_TILE,),
        in_specs=[row(d), _resident((1, d)), _resident(w.shape), _resident((1, 2 * c))],
        out_specs=row(c),
        out_shape=jax.ShapeDtypeStruct((m, c), F32),
        compiler_params=_params("parallel"),
        name="conformer_glu",
    )(x, nw.reshape(1, d), w.astype(BF16), b.reshape(1, 2 * c))


CONV_HALO = 32


def _dwconv_kernel(x_ref, u_ref, halo_ref, wdw_ref, bdw_ref, lnw_ref, lnb_ref, w2_ref, b2_ref,
                   o_ref, upad_ref, shift_ref, *, width):
    t = pl.program_id(1)
    tt = u_ref.shape[0]

    @pl.when(t == 0)
    def _():
        upad_ref[0:CONV_HALO, :] = jnp.zeros((CONV_HALO, upad_ref.shape[1]), F32)

    @pl.when(t > 0)
    def _():
        upad_ref[0:CONV_HALO, :] = halo_ref[...]

    upad_ref[CONV_HALO:, :] = u_ref[...]
    span = tt + CONV_HALO - V7X_SUBLANES
    for s in range(1, V7X_SUBLANES):
        shift_ref[s - 1, 0:span, :] = upad_ref[pl.ds(s, span), :]
    first = CONV_HALO - (width - 1)
    acc = bdw_ref[...]
    for k in range(width):
        tile_row, s = divmod(first + k, V7X_SUBLANES)
        start = tile_row * V7X_SUBLANES
        src = upad_ref[pl.ds(start, tt), :] if s == 0 else shift_ref[s - 1, pl.ds(start, tt), :]
        acc = acc + wdw_ref[k:k + 1, :] * src
    mu = jnp.mean(acc, axis=-1, keepdims=True)
    xc = acc - mu
    y = xc * lax.rsqrt(jnp.mean(xc * xc, axis=-1, keepdims=True) + EPS) * lnw_ref[...] + lnb_ref[...]
    y = _silu(y).astype(BF16)
    o_ref[...] = x_ref[...] + _dot(y, w2_ref[...]) + b2_ref[...]


def _dwconv(x, u, w_dw, b_dw, ln_w, ln_b, w2, b2, batch, seq):
    m, d = x.shape
    c = u.shape[1]
    width = w_dw.shape[0]
    nt = seq // SEQ_TILE
    halo_blocks = SEQ_TILE // CONV_HALO
    row_of = lambda b, t: (b * nt + t, 0)
    vec = lambda p: p.astype(F32).reshape(1, -1)
    return pl.pallas_call(
        functools.partial(_dwconv_kernel, width=width),
        grid=(batch, nt),
        in_specs=[pl.BlockSpec((SEQ_TILE, d), row_of),
                  pl.BlockSpec((SEQ_TILE, c), row_of),
                  pl.BlockSpec((CONV_HALO, c),
                               lambda b, t: (jnp.maximum((b * nt + t) * halo_blocks - 1, 0), 0)),
                  _resident((width, c)), _resident((1, c)), _resident((1, c)), _resident((1, c)),
                  _resident((c, d)), _resident((1, d))],
        out_specs=pl.BlockSpec((SEQ_TILE, d), row_of),
        out_shape=jax.ShapeDtypeStruct((m, d), F32),
        scratch_shapes=[pltpu.VMEM((SEQ_TILE + CONV_HALO, c), F32),
                        pltpu.VMEM((V7X_SUBLANES - 1, SEQ_TILE + CONV_HALO, c), F32)],
        compiler_params=_params("parallel", "parallel"),
        name="conformer_dwconv",
    )(x, u, u, w_dw.astype(F32), vec(b_dw), vec(ln_w), vec(ln_b), w2.astype(BF16), vec(b2))


def kernel(x, norm_w, ffn_w_gate, ffn_w_up, ffn_w_down, mix_w_in, dn_conv_w, attn_sinks, dn_a_log, dn_dt_bias, dn_norm_w, mix_w_out, conv_w_pw1, conv_b_pw1, conv_w_dw, conv_b_dw, conv_ln_w, conv_ln_b, conv_w_pw2, conv_b_pw2, final_norm_w):
    batch, seq, d = x.shape
    depth = norm_w.shape[0]
    assert seq % SEQ_TILE == 0 and seq % ATTN_BLOCK == 0 and (batch * seq) % ROW_TILE == 0
    assert conv_w_dw.shape[1] <= CONV_HALO + 1 and WINDOW <= ATTN_BLOCK
    assert mix_w_in.shape[2] == MAIN_COLS + 2 * DN_HEADS
    h = x.reshape(batch * seq, d)
    for layer in range(depth):
        h = _ffn(h, norm_w[layer, 0], ffn_w_gate[layer, 0], ffn_w_up[layer, 0], ffn_w_down[layer, 0])
        if layer % 2 == 0:
            e = layer // 2
            qa, kva, qkvb, z, gates = _inproj(h, norm_w[layer, 1], mix_w_in[e])
            att = _attention(qa, kva, attn_sinks[e], batch, seq)
            dn = _deltanet(qkvb, z, gates, dn_conv_w[e], dn_a_log[e], dn_dt_bias[e], dn_norm_w[e],
                           batch, seq)
            h = _outproj(h, att, dn, mix_w_out[e])
        else:
            c = layer // 2
            u = _glu(h, norm_w[layer, 1], conv_w_pw1[c], conv_b_pw1[c])
            h = _dwconv(h, u, conv_w_dw[c], conv_b_dw[c], conv_ln_w[c], conv_ln_b[c],
                        conv_w_pw2[c], conv_b_pw2[c], batch, seq)
        last = layer == depth - 1
        h = _ffn(h, norm_w[layer, 2], ffn_w_gate[layer, 1], ffn_w_up[layer, 1], ffn_w_down[layer, 1],
                 final_w=final_norm_w if last else None)
    return h.reshape(batch, seq, d)
```

```python
import functools

import jax
import jax.numpy as jnp
from jax import lax
from jax.experimental import pallas as pl
from jax.experimental.pallas import tpu as pltpu

F32 = jnp.float32
BF16 = jnp.bfloat16

ATTN_HEADS = 8
ATTN_KV_HEADS = 2
HEAD_DIM = 64
WINDOW = 128
ATTN_BLOCK = 128
DN_HEADS = 8
DN_DK = 64
DN_DV = 64
DN_CONV = 4
CHUNK = 64
EPS = 1e-6

V7X_LANES = 128
V7X_SUBLANES = 8
V7X_VMEM_LIMIT_BYTES = 56 * 1024 * 1024

ROW_TILE = 512
FFN_ROW_TILE = 1024
FFN_COL_TILE = 256
SEQ_TILE = 256
GATE_COLS = V7X_LANES


def _params(*sem):
    return pltpu.CompilerParams(dimension_semantics=sem, vmem_limit_bytes=V7X_VMEM_LIMIT_BYTES)


def _rms(x, w):
    return x * lax.rsqrt(jnp.mean(x * x, axis=-1, keepdims=True) + EPS) * w


def _silu(x):
    return x * (1.0 / (1.0 + jnp.exp(-x)))


def _softplus(x):
    return jnp.maximum(x, 0.0) + jnp.log(1.0 + jnp.exp(-jnp.abs(x)))


def _dot(a, b):
    return jnp.dot(a, b, preferred_element_type=F32)


def _resident(shape):
    return pl.BlockSpec(shape, lambda *_: (0,) * len(shape))


def _layer_block(shape, index):
    return pl.BlockSpec((None,) + tuple(shape), lambda *_: (index,) + (0,) * len(shape))


def _rows_after_prep(n_rows, n_cols):
    return pl.BlockSpec((n_rows, n_cols), lambda i: (jnp.maximum(i - 1, 0), 0))


def _ffn_kernel(x_ref, nw_ref, wg_ref, wu_ref, wd_ref, *rest, final_norm, n_prep):
    if final_norm:
        fw_ref, o_ref, wg16_ref, wu16_ref, wd16_ref, xn_ref, h_ref = rest
    else:
        o_ref, wg16_ref, wu16_ref, wd16_ref, xn_ref, h_ref = rest
    i = pl.program_id(0)

    @pl.when(i < n_prep)
    def _():
        wg16_ref[i] = wg_ref[...].astype(BF16)
        wu16_ref[i] = wu_ref[...].astype(BF16)
        wd16_ref[pl.ds(pl.multiple_of(i * FFN_COL_TILE, FFN_COL_TILE), FFN_COL_TILE), :] = (
            wd_ref[...].astype(BF16))

    @pl.when(i >= n_prep)
    def _():
        x = x_ref[...]
        xn_ref[...] = _rms(x, nw_ref[...]).astype(BF16)
        for c in range(n_prep):
            g = _dot(xn_ref[...], wg16_ref[c])
            u = _dot(xn_ref[...], wu16_ref[c])
            h_ref[:, c * FFN_COL_TILE:(c + 1) * FFN_COL_TILE] = (_silu(g) * u).astype(BF16)
        y = x + 0.5 * _dot(h_ref[...], wd16_ref[...])
        if final_norm:
            y = _rms(y, fw_ref[...])
        o_ref[...] = y


def _ffn(x, norm_w, w_gate, w_up, w_down, layer, half, final_w=None):
    m, d = x.shape
    d_ff = w_gate.shape[-1]
    n_prep = d_ff // FFN_COL_TILE
    norm_row = 3 * layer + 2 * half
    row = pl.BlockSpec((FFN_ROW_TILE, d), lambda i: (jnp.maximum(i - n_prep, 0), 0))
    tile_of = lambda i: jnp.minimum(i, n_prep - 1)
    in_specs = [row,
                pl.BlockSpec((None, 1, d), lambda i: (norm_row, 0, 0)),
                pl.BlockSpec((None, None, d, FFN_COL_TILE), lambda i: (layer, half, 0, tile_of(i))),
                pl.BlockSpec((None, None, d, FFN_COL_TILE), lambda i: (layer, half, 0, tile_of(i))),
                pl.BlockSpec((None, None, FFN_COL_TILE, d), lambda i: (layer, half, tile_of(i), 0))]
    args = [x, norm_w.reshape(-1, 1, d), w_gate, w_up, w_down]
    if final_w is not None:
        in_specs.append(_resident((1, d)))
        args.append(final_w.reshape(1, d))
    return pl.pallas_call(
        functools.partial(_ffn_kernel, final_norm=final_w is not None, n_prep=n_prep),
        grid=(n_prep + m // FFN_ROW_TILE,),
        in_specs=in_specs,
        out_specs=row,
        out_shape=jax.ShapeDtypeStruct((m, d), F32),
        scratch_shapes=[pltpu.VMEM((n_prep, d, FFN_COL_TILE), BF16),
                        pltpu.VMEM((n_prep, d, FFN_COL_TILE), BF16),
                        pltpu.VMEM((d_ff, d), BF16),
                        pltpu.VMEM((FFN_ROW_TILE, d), BF16),
                        pltpu.VMEM((FFN_ROW_TILE, d_ff), BF16)],
        compiler_params=_params("arbitrary"),
        name="ffn",
    )(*args)


Q_A = ATTN_HEADS * HEAD_DIM
KV_A = ATTN_KV_HEADS * HEAD_DIM
QK_B = DN_HEADS * DN_DK
V_B = DN_HEADS * DN_DV
QKV_B = 2 * QK_B + V_B
MAIN_COLS = Q_A + 2 * KV_A + QKV_B + V_B


def _inproj_kernel(x_ref, nw_ref, w_ref, qa_ref, kva_ref, qkvb_ref, z_ref, gates_ref, w16_ref, wgate16_ref):
    i = pl.program_id(0)

    @pl.when(i == 0)
    def _():
        w16_ref[...] = w_ref[:, :MAIN_COLS].astype(BF16)
        wgate16_ref[...] = jnp.zeros_like(wgate16_ref)
        wgate16_ref[:, :2 * DN_HEADS] = w_ref[:, MAIN_COLS:].astype(BF16)

    @pl.when(i > 0)
    def _():
        xn = _rms(x_ref[...], nw_ref[...]).astype(BF16)
        c0, c1, c2 = Q_A, Q_A + 2 * KV_A, Q_A + 2 * KV_A + QKV_B
        qa_ref[...] = _dot(xn, w16_ref[:, :c0]).astype(BF16)
        kva_ref[...] = _dot(xn, w16_ref[:, c0:c1])
        qkvb_ref[...] = _dot(xn, w16_ref[:, c1:c2])
        z_ref[...] = _dot(xn, w16_ref[:, c2:])
        gates_ref[...] = _dot(xn, wgate16_ref[...])


def _inproj(x, norm_w, w_in, layer, e):
    m, d = x.shape
    row = lambda n: _rows_after_prep(ROW_TILE, n)
    return pl.pallas_call(
        _inproj_kernel,
        grid=(1 + m // ROW_TILE,),
        in_specs=[row(d), _layer_block((1, d), 3 * layer + 1), _layer_block(w_in.shape[1:], e)],
        out_specs=[row(Q_A), row(2 * KV_A), row(QKV_B), row(V_B), row(GATE_COLS)],
        out_shape=[jax.ShapeDtypeStruct((m, Q_A), BF16),
                   jax.ShapeDtypeStruct((m, 2 * KV_A), F32),
                   jax.ShapeDtypeStruct((m, QKV_B), F32),
                   jax.ShapeDtypeStruct((m, V_B), F32),
                   jax.ShapeDtypeStruct((m, GATE_COLS), F32)],
        scratch_shapes=[pltpu.VMEM((d, MAIN_COLS), BF16), pltpu.VMEM((d, GATE_COLS), BF16)],
        compiler_params=_params("arbitrary"),
        name="mixer_inproj",
    )(x, norm_w.reshape(-1, 1, d), w_in)


def _alibi_slopes(n_heads):
    return [float(2.0 ** (-8.0 * (h + 1) / n_heads)) for h in range(n_heads)]


ATTN_Q_TILE = 512


def _attn_kernel(sinks_ref, q_ref, kv_ref, kv_prev_ref, o_ref, *, sink_row):
    t = pl.program_id(1)
    half = lax.broadcasted_iota(jnp.int32, (1, V7X_LANES), 1) < HEAD_DIM
    kv = jnp.concatenate([kv_prev_ref[...], kv_ref[...]], axis=0)

    def placed(tile):
        swapped = pltpu.roll(tile, HEAD_DIM, axis=1)
        keep_lo = lambda x: jnp.where(half, x, 0.0).astype(BF16)
        keep_hi = lambda x: jnp.where(half, 0.0, x).astype(BF16)
        return {(0, 0): keep_lo(tile), (1, 1): keep_hi(tile), (0, 1): keep_hi(swapped), (1, 0): keep_lo(swapped)}

    k_at = placed(kv[:, :KV_A])
    v_at = placed(kv[:, KV_A:])

    i = lax.broadcasted_iota(jnp.int32, (ATTN_BLOCK, 2 * ATTN_BLOCK), 0)
    j = lax.broadcasted_iota(jnp.int32, (ATTN_BLOCK, 2 * ATTN_BLOCK), 1)
    dist = i + ATTN_BLOCK - j
    valid = jnp.where(dist >= 0, jnp.where(dist < WINDOW, 1, 0), 0) > 0
    no_prev = jnp.where(j < ATTN_BLOCK, jnp.where(t == 0, 1, 0), 0) > 0
    distf = dist.astype(F32)
    slopes = _alibi_slopes(ATTN_HEADS)
    group = ATTN_HEADS // ATTN_KV_HEADS
    bias = [jnp.where(valid, -slopes[h] * distf, -1e30) for h in range(ATTN_HEADS)]
    q = q_ref[...] * (HEAD_DIM ** -0.5)

    for qb in range(ATTN_Q_TILE // ATTN_BLOCK):
        rows = slice(qb * ATTN_BLOCK, (qb + 1) * ATTN_BLOCK)
        win = slice(qb * ATTN_BLOCK, (qb + 2) * ATTN_BLOCK)
        tiles = []
        for pair in range(ATTN_HEADS // 2):
            q_pair = q[rows, pair * V7X_LANES:(pair + 1) * V7X_LANES]
            o_pair = None
            for par in range(2):
                h = 2 * pair + par
                kh = h // group
                b_h = jnp.where(no_prev, -1e30, bias[h]) if qb == 0 else bias[h]
                s = lax.dot_general(q_pair, k_at[kh, par][win], (((1,), (1,)), ((), ())),
                                    preferred_element_type=F32) + b_h
                sink = sinks_ref[sink_row, h]
                mx = jnp.maximum(jnp.max(s, axis=-1, keepdims=True), sink)
                e = jnp.exp(s - mx)
                inv = 1.0 / (jnp.sum(e, axis=-1, keepdims=True) + jnp.exp(sink - mx))
                o_h = _dot(e.astype(BF16), v_at[kh, par][win]) * inv
                o_pair = o_h if o_pair is None else o_pair + o_h
            tiles.append(o_pair)
        o_ref[rows, :] = jnp.concatenate(tiles, axis=1).astype(BF16)


def _attention(qa, kva, sinks, e, batch, seq):
    assert KV_A == V7X_LANES and ATTN_KV_HEADS == 2 and seq % ATTN_Q_TILE == 0
    nt = seq // ATTN_Q_TILE
    prev_blocks = ATTN_Q_TILE // ATTN_BLOCK
    row_of = lambda b, t: (b * nt + t, 0)
    return pl.pallas_call(
        functools.partial(_attn_kernel, sink_row=e),
        grid=(batch, nt),
        in_specs=[pl.BlockSpec(memory_space=pltpu.SMEM),
                  pl.BlockSpec((ATTN_Q_TILE, Q_A), row_of),
                  pl.BlockSpec((ATTN_Q_TILE, 2 * KV_A), row_of),
                  pl.BlockSpec((ATTN_BLOCK, 2 * KV_A),
                               lambda b, t: (jnp.maximum((b * nt + t) * prev_blocks - 1, 0), 0))],
        out_specs=pl.BlockSpec((ATTN_Q_TILE, Q_A), row_of),
        out_shape=jax.ShapeDtypeStruct(qa.shape, BF16),
        compiler_params=_params("parallel", "parallel"),
        name="swa_attention",
    )(sinks.astype(F32), qa, kva, kva)


PAIR = V7X_LANES // DN_DK
N_PAIRS = DN_HEADS // PAIR


def _split2(x):
    hi = x.astype(BF16)
    return hi, (x - hi.astype(F32)).astype(BF16)


def _split3(x):
    hi = x.astype(BF16)
    rest = x - hi.astype(F32)
    mid = rest.astype(BF16)
    return hi, mid, (rest - mid.astype(F32)).astype(BF16)


def _select_sum(x, onehot, splitter=_split3):
    return sum(_dot(part, onehot) for part in splitter(x))


def _group_sums(x, ones_bd):
    hi, lo = _split2(x)
    cols = []
    for j in range(x.shape[1] // V7X_LANES):
        sl = slice(j * V7X_LANES, (j + 1) * V7X_LANES)
        cols.append(_dot(hi[:, sl], ones_bd) + _dot(lo[:, sl], ones_bd))
    return jnp.concatenate(cols, axis=1)


def _bdot(a, b):
    return lax.dot_general(a, b, (((2,), (1,)), ((0,), (0,))), preferred_element_type=F32)


def _bdot_nt(a, b):
    return lax.dot_general(a, b, (((2,), (2,)), ((0,), (0,))), preferred_element_type=F32)


def _blockdiag(x16, bd16):
    return jnp.concatenate([x16, x16], axis=1) * bd16


def _neumann_lower(low, bd16):
    m = -low
    m16 = m.astype(BF16)
    n = m
    m = _bdot(m16, _blockdiag(m16, bd16))
    span = 2
    while span < CHUNK:
        m16 = m.astype(BF16)
        if 2 * span >= CHUNK:
            n = n + m + _bdot(n.astype(BF16), _blockdiag(m16, bd16))
        else:
            p = _bdot(jnp.concatenate([n.astype(BF16), m16], axis=1), _blockdiag(m16, bd16))
            n = n + m + p[:, :CHUNK]
            m = p[:, CHUNK:]
        span *= 2
    return n


def _deltanet_kernel(qkv_ref, halo_ref, z_ref, gates_ref, convw_ref, alog_ref, dt_ref, normw_ref,
                     o_ref, state_ref, xpad_ref):
    t = pl.program_id(1)
    tt = qkv_ref.shape[0]
    n_chunks = tt // CHUNK

    @pl.when(t == 0)
    def _():
        state_ref[...] = jnp.zeros_like(state_ref)
        xpad_ref[0:V7X_SUBLANES, :] = jnp.zeros((V7X_SUBLANES, QKV_B), F32)

    @pl.when(t > 0)
    def _():
        xpad_ref[0:V7X_SUBLANES, :] = halo_ref[...]

    xpad_ref[V7X_SUBLANES:, :] = qkv_ref[...]
    first = V7X_SUBLANES - (DN_CONV - 1)
    acc = convw_ref[DN_CONV - 1:DN_CONV, :] * xpad_ref[pl.ds(V7X_SUBLANES, tt), :]
    for k in range(DN_CONV - 1):
        acc = acc + convw_ref[k:k + 1, :] * xpad_ref[pl.ds(first + k, tt), :]
    xc = _silu(acc)

    def iota2(shape, axis):
        return lax.broadcasted_iota(jnp.int32, shape, axis)

    sq = (V7X_LANES, V7X_LANES)
    same_head = (iota2(sq, 0) // DN_DK) == (iota2(sq, 1) // DN_DK)
    bd16 = jnp.where(same_head, 1.0, 0.0).astype(BF16)

    qk = xc[:, :2 * QK_B]
    qk = qk * lax.rsqrt(_group_sums(qk * qk, bd16) + EPS)
    q_all = qk[:, :QK_B] * (DN_DK ** -0.5)
    k_all = qk[:, QK_B:]
    v_all = xc[:, 2 * QK_B:]

    gates = gates_ref[...]
    beta_g = 1.0 / (1.0 + jnp.exp(-gates))
    g_g = -jnp.exp(alog_ref[...]) * _softplus(gates + dt_ref[...])
    same_chunk = (iota2((tt, tt), 0) // CHUNK) == (iota2((tt, tt), 1) // CHUNK)
    tril_bd = jnp.where(same_chunk, jnp.where(iota2((tt, tt), 0) >= iota2((tt, tt), 1), 1.0, 0.0),
                        0.0).astype(BF16)
    gc_g = sum(_dot(tril_bd, part) for part in _split3(g_g))
    ex = (GATE_COLS, V_B)
    spread_beta = jnp.where(iota2(ex, 0) == iota2(ex, 1) // DN_DK, 1.0, 0.0).astype(BF16)
    spread_g = jnp.where(iota2(ex, 0) == iota2(ex, 1) // DN_DK + DN_HEADS, 1.0, 0.0).astype(BF16)
    beta_x = _select_sum(beta_g, spread_beta)
    gc_x = _select_sum(gc_g, spread_g)

    egc_x = jnp.exp(gc_x)
    kb_all = k_all * beta_x
    vb_all = v_all * beta_x
    kbe_all = kb_all * egc_x
    qd_all = q_all * egc_x

    units = [(c, p) for c in range(n_chunks) for p in range(N_PAIRS)]
    packed = lambda x: jnp.stack(
        [x[c * CHUNK:(c + 1) * CHUNK, p * V7X_LANES:(p + 1) * V7X_LANES] for c, p in units], axis=0)
    gc = packed(gc_x)
    g_last = gc[:, CHUNK - 1:CHUNK, :]
    k = packed(k_all)
    k_dec_t = jnp.swapaxes(k * jnp.exp(g_last - gc), 1, 2).astype(BF16)
    state_gain = jnp.exp(g_last)

    ii = iota2((CHUNK, V7X_LANES), 0)
    jj = iota2((CHUNK, V7X_LANES), 1) % CHUNK
    causal = (ii >= jj)[None]
    strict = (ii > jj)[None]
    gc_row = jnp.sum(jnp.where((ii == jj)[None], gc, 0.0), axis=1, keepdims=True)
    decay = jnp.where(causal, jnp.exp(jnp.where(causal, gc - gc_row, 0.0)), 0.0)

    k_bd = _blockdiag(k.astype(BF16), bd16)
    kq = jnp.concatenate([packed(kb_all).astype(BF16), packed(q_all).astype(BF16)], axis=1)
    kk = _bdot_nt(kq, k_bd)
    low = jnp.where(strict, kk[:, :CHUNK] * decay, 0.0)
    attn = (kk[:, CHUNK:] * decay).astype(BF16)

    n16 = _neumann_lower(low, bd16).astype(BF16)
    vb = packed(vb_all)
    kbe = packed(kbe_all)
    rhs = jnp.concatenate([_blockdiag(vb.astype(BF16), bd16), _blockdiag(kbe.astype(BF16), bd16)], axis=2)
    uw = _bdot(n16, rhs)
    u = vb + uw[:, :, :V7X_LANES]
    w = kbe + uw[:, :, V7X_LANES:]
    state_lhs = jnp.concatenate([w.astype(BF16), packed(qd_all).astype(BF16)], axis=1)

    low_lanes = iota2((DN_DK, V7X_LANES), 1) < DN_DV
    s = state_ref[...]
    outs = []
    for c in range(n_chunks):
        g = slice(c * N_PAIRS, (c + 1) * N_PAIRS)
        ps = _bdot(state_lhs[g], _blockdiag(s.astype(BF16), bd16))
        v_new = (u[g] - ps[:, :CHUNK]).astype(BF16)
        outs.append(ps[:, CHUNK:] + _bdot(attn[g], _blockdiag(v_new, bd16)))
        upd = _bdot(k_dec_t[g], v_new)
        s = s * state_gain[g] + jnp.where(low_lanes, upd[:, :DN_DK], upd[:, DN_DK:])
    state_ref[...] = s

    o_all = jnp.concatenate(
        [jnp.concatenate([o_c[p] for p in range(N_PAIRS)], axis=1) for o_c in outs], axis=0)
    o_all = o_all * lax.rsqrt(_group_sums(o_all * o_all, bd16) * (1.0 / DN_DV) + EPS)
    o_ref[...] = (o_all * normw_ref[...] * _silu(z_ref[...])).astype(BF16)


def _deltanet(qkvb, z, gates, conv_w, a_log, dt_bias, norm_w, e, batch, seq):
    m = qkvb.shape[0]
    nt = seq // SEQ_TILE
    n_even = conv_w.shape[0]
    row_of = lambda b, t: (b * nt + t, 0)
    halo_blocks = SEQ_TILE // V7X_SUBLANES
    pad_row = lambda p: jnp.pad(p.astype(F32), ((0, 0), (DN_HEADS, GATE_COLS - 2 * DN_HEADS))).reshape(
        n_even, 1, GATE_COLS)
    return pl.pallas_call(
        _deltanet_kernel,
        grid=(batch, nt),
        in_specs=[pl.BlockSpec((SEQ_TILE, QKV_B), row_of),
                  pl.BlockSpec((V7X_SUBLANES, QKV_B),
                               lambda b, t: (jnp.maximum((b * nt + t) * halo_blocks - 1, 0), 0)),
                  pl.BlockSpec((SEQ_TILE, V_B), row_of),
                  pl.BlockSpec((SEQ_TILE, GATE_COLS), row_of),
                  _layer_block((DN_CONV, QKV_B), e),
                  _layer_block((1, GATE_COLS), e), _layer_block((1, GATE_COLS), e),
                  _layer_block((1, V_B), e)],
        out_specs=pl.BlockSpec((SEQ_TILE, V_B), row_of),
        out_shape=jax.ShapeDtypeStruct((m, V_B), BF16),
        scratch_shapes=[pltpu.VMEM((N_PAIRS, DN_DK, V7X_LANES), F32),
                        pltpu.VMEM((SEQ_TILE + V7X_SUBLANES, QKV_B), F32)],
        compiler_params=_params("parallel", "arbitrary"),
        name="gated_deltanet",
    )(qkvb, qkvb, z, gates, conv_w, pad_row(a_log), pad_row(dt_bias),
      jnp.tile(norm_w.astype(F32), (1, DN_HEADS)).reshape(n_even, 1, V_B))


def _outproj_kernel(x_ref, att_ref, dn_ref, w_ref, o_ref, w16_ref):
    i = pl.program_id(0)

    @pl.when(i == 0)
    def _():
        w16_ref[...] = w_ref[...].astype(BF16)

    @pl.when(i > 0)
    def _():
        o_ref[...] = (x_ref[...] + _dot(att_ref[...], w16_ref[:Q_A, :])
                      + _dot(dn_ref[...], w16_ref[Q_A:, :]))


def _outproj(x, att, dn, w_out, e):
    m, d = x.shape
    row = lambda n: _rows_after_prep(ROW_TILE, n)
    return pl.pallas_call(
        _outproj_kernel,
        grid=(1 + m // ROW_TILE,),
        in_specs=[row(d), row(Q_A), row(V_B), _layer_block(w_out.shape[1:], e)],
        out_specs=row(d),
        out_shape=jax.ShapeDtypeStruct((m, d), F32),
        scratch_shapes=[pltpu.VMEM(w_out.shape[1:], BF16)],
        compiler_params=_params("arbitrary"),
        name="mixer_outproj",
    )(x, att, dn, w_out)


def _glu_kernel(x_ref, nw_ref, w_ref, b_ref, o_ref, w16_ref):
    i = pl.program_id(0)

    @pl.when(i == 0)
    def _():
        w16_ref[...] = w_ref[...].astype(BF16)

    @pl.when(i > 0)
    def _():
        xn = _rms(x_ref[...], nw_ref[...]).astype(BF16)
        c = o_ref.shape[1]
        a = _dot(xn, w16_ref[:, :c]) + b_ref[:, :c]
        g = _dot(xn, w16_ref[:, c:]) + b_ref[:, c:]
        o_ref[...] = a * (1.0 / (1.0 + jnp.exp(-g)))


def _glu(x, norm_w, w, b, layer, ci):
    m, d = x.shape
    c = w.shape[2] // 2
    row = lambda n: _rows_after_prep(ROW_TILE, n)
    return pl.pallas_call(
        _glu_kernel,
        grid=(1 + m // ROW_TILE,),
        in_specs=[row(d), _layer_block((1, d), 3 * layer + 1), _layer_block(w.shape[1:], ci),
                  _layer_block((1, 2 * c), ci)],
        out_specs=row(c),
        out_shape=jax.ShapeDtypeStruct((m, c), F32),
        scratch_shapes=[pltpu.VMEM(w.shape[1:], BF16)],
        compiler_params=_params("arbitrary"),
        name="conformer_glu",
    )(x, norm_w.reshape(-1, 1, d), w, b.reshape(b.shape[0], 1, 2 * c))


CONV_HALO = 32

def _dwconv_kernel(x_ref, u_ref, halo_ref, wdw_ref, bdw_ref, lnw_ref, lnb_ref, w2_ref, b2_ref,
                   o_ref, upad_ref, shift_ref, w2_16_ref, *, width, tiles_per_seq):
    i = pl.program_id(0)

    @pl.when(i == 0)
    def _():
        w2_16_ref[...] = w2_ref[...].astype(BF16)

    @pl.when(i > 0)
    def _():
        _dwconv_tile(x_ref, u_ref, halo_ref, wdw_ref, bdw_ref, lnw_ref, lnb_ref, w2_16_ref, b2_ref,
                     o_ref, upad_ref, shift_ref, width=width, first_of_seq=(i - 1) % tiles_per_seq == 0)


def _dwconv_tile(x_ref, u_ref, halo_ref, wdw_ref, bdw_ref, lnw_ref, lnb_ref, w2_ref, b2_ref,
                 o_ref, upad_ref, shift_ref, *, width, first_of_seq):
    tt = u_ref.shape[0]

    @pl.when(first_of_seq)
    def _():
        upad_ref[0:CONV_HALO, :] = jnp.zeros((CONV_HALO, upad_ref.shape[1]), F32)

    @pl.when(jnp.logical_not(first_of_seq))
    def _():
        upad_ref[0:CONV_HALO, :] = halo_ref[...]

    upad_ref[CONV_HALO:, :] = u_ref[...]
    span = tt + CONV_HALO - V7X_SUBLANES
    for s in range(1, V7X_SUBLANES):
        shift_ref[s - 1, 0:span, :] = upad_ref[pl.ds(s, span), :]
    first = CONV_HALO - (width - 1)

    def tap_rows(k):
        tile_row, s = divmod(first + k, V7X_SUBLANES)
        start = tile_row * V7X_SUBLANES
        return upad_ref[pl.ds(start, tt), :] if s == 0 else shift_ref[s - 1, pl.ds(start, tt), :]

    acc = bdw_ref[...]
    for k in range(width):
        acc = acc + wdw_ref[k:k + 1, :] * tap_rows(k)
    mu = jnp.mean(acc, axis=-1, keepdims=True)
    xc = acc - mu
    y = xc * lax.rsqrt(jnp.mean(xc * xc, axis=-1, keepdims=True) + EPS) * lnw_ref[...] + lnb_ref[...]
    y = _silu(y).astype(BF16)
    o_ref[...] = x_ref[...] + _dot(y, w2_ref[...]) + b2_ref[...]


def _dwconv(x, u, w_dw, b_dw, ln_w, ln_b, w2, b2, ci, seq):
    m, d = x.shape
    c = u.shape[1]
    width = w_dw.shape[1]
    halo_blocks = SEQ_TILE // CONV_HALO
    vec = lambda p: p.reshape(p.shape[0], 1, p.shape[1])
    return pl.pallas_call(
        functools.partial(_dwconv_kernel, width=width, tiles_per_seq=seq // SEQ_TILE),
        grid=(1 + m // SEQ_TILE,),
        in_specs=[_rows_after_prep(SEQ_TILE, d),
                  _rows_after_prep(SEQ_TILE, c),
                  pl.BlockSpec((CONV_HALO, c), lambda i: (jnp.maximum((i - 1) * halo_blocks - 1, 0), 0)),
                  _layer_block((width, c), ci), _layer_block((1, c), ci), _layer_block((1, c), ci),
                  _layer_block((1, c), ci), _layer_block((c, d), ci), _layer_block((1, d), ci)],
        out_specs=_rows_after_prep(SEQ_TILE, d),
        out_shape=jax.ShapeDtypeStruct((m, d), F32),
        scratch_shapes=[pltpu.VMEM((SEQ_TILE + CONV_HALO, c), F32),
                        pltpu.VMEM((V7X_SUBLANES - 1, SEQ_TILE + CONV_HALO, c), F32),
                        pltpu.VMEM((c, d), BF16)],
        compiler_params=_params("arbitrary"),
        name="conformer_dwconv",
    )(x, u, u, w_dw, vec(b_dw), vec(ln_w), vec(ln_b), w2, vec(b2))


def kernel(x, norm_w, ffn_w_gate, ffn_w_up, ffn_w_down, mix_w_in, dn_conv_w, attn_sinks, dn_a_log, dn_dt_bias, dn_norm_w, mix_w_out, conv_w_pw1, conv_b_pw1, conv_w_dw, conv_b_dw, conv_ln_w, conv_ln_b, conv_w_pw2, conv_b_pw2, final_norm_w):
    batch, seq, d = x.shape
    depth = norm_w.shape[0]
    assert seq % SEQ_TILE == 0 and (batch * seq) % ROW_TILE == 0 and (batch * seq) % FFN_ROW_TILE == 0
    assert ffn_w_gate.shape[-1] % FFN_COL_TILE == 0
    assert conv_w_dw.shape[1] <= CONV_HALO + 1 and WINDOW <= ATTN_BLOCK
    assert mix_w_in.shape[2] == MAIN_COLS + 2 * DN_HEADS
    h = x.reshape(batch * seq, d)
    for layer in range(depth):
        h = _ffn(h, norm_w, ffn_w_gate, ffn_w_up, ffn_w_down, layer, 0)
        if layer % 2 == 0:
            e = layer // 2
            qa, kva, qkvb, z, gates = _inproj(h, norm_w, mix_w_in, layer, e)
            att = _attention(qa, kva, attn_sinks, e, batch, seq)
            dn = _deltanet(qkvb, z, gates, dn_conv_w, dn_a_log, dn_dt_bias, dn_norm_w, e, batch, seq)
            h = _outproj(h, att, dn, mix_w_out, e)
        else:
            c = layer // 2
            u = _glu(h, norm_w, conv_w_pw1, conv_b_pw1, layer, c)
            h = _dwconv(h, u, conv_w_dw, conv_b_dw, conv_ln_w, conv_ln_b, conv_w_pw2, conv_b_pw2, c, seq)
        last = layer == depth - 1
        h = _ffn(h, norm_w, ffn_w_gate, ffn_w_up, ffn_w_down, layer, 1,
                 final_w=final_norm_w if last else None)
    return h.reshape(batch, seq, d)
```

```python
import functools

import jax
import jax.numpy as jnp
from jax import lax
from jax.experimental import pallas as pl
from jax.experimental.pallas import tpu as pltpu

F32 = jnp.float32
BF16 = jnp.bfloat16

ATTN_HEADS = 8
ATTN_KV_HEADS = 2
HEAD_DIM = 64
WINDOW = 128
ATTN_BLOCK = 128
DN_HEADS = 8
DN_DK = 64
DN_DV = 64
DN_CONV = 4
CHUNK = 64
EPS = 1e-6

V7X_LANES = 128
V7X_SUBLANES = 8
V7X_VMEM_LIMIT_BYTES = 56 * 1024 * 1024

ROW_TILE = 1024
FFN_ROW_TILE = 1024
FFN_COL_TILE = 256
SEQ_TILE = 256
GATE_COLS = V7X_LANES


def _params(*sem):
    return pltpu.CompilerParams(dimension_semantics=sem, vmem_limit_bytes=V7X_VMEM_LIMIT_BYTES)


def _rms(x, w):
    return x * lax.rsqrt(jnp.mean(x * x, axis=-1, keepdims=True) + EPS) * w


def _silu(x):
    return x * (1.0 / (1.0 + jnp.exp(-x)))


def _softplus(x):
    return jnp.maximum(x, 0.0) + jnp.log(1.0 + jnp.exp(-jnp.abs(x)))


def _dot(a, b):
    return jnp.dot(a, b, preferred_element_type=F32)


def _resident(shape):
    return pl.BlockSpec(shape, lambda *_: (0,) * len(shape))


def _layer_block(shape, index):
    return pl.BlockSpec((None,) + tuple(shape), lambda *_: (index,) + (0,) * len(shape),
                        pipeline_mode=pl.Buffered(1))


def _rows_after_prep(n_rows, n_cols):
    return pl.BlockSpec((n_rows, n_cols), lambda i: (jnp.maximum(i - 1, 0), 0))


def _ffn_kernel(x_ref, nw_ref, wg_ref, wu_ref, wd_ref, *rest, final_norm, n_prep):
    if final_norm:
        fw_ref, o_ref, wg16_ref, wu16_ref, wd16_ref, xn_ref, h_ref, acc_ref = rest
    else:
        o_ref, wg16_ref, wu16_ref, wd16_ref, xn_ref, h_ref, acc_ref = rest
    i = pl.program_id(0)

    def finish(ffn_out):
        y = x_ref[...] + 0.5 * ffn_out
        if final_norm:
            y = _rms(y, fw_ref[...])
        o_ref[...] = y

    @pl.when(i < n_prep)
    def _():
        @pl.when(i == 0)
        def _():
            xn_ref[...] = _rms(x_ref[...], nw_ref[...]).astype(BF16)
            acc_ref[...] = jnp.zeros_like(acc_ref)

        wg, wu, wd = wg_ref[...].astype(BF16), wu_ref[...].astype(BF16), wd_ref[...].astype(BF16)
        wg16_ref[i] = wg
        wu16_ref[i] = wu
        wd16_ref[pl.ds(pl.multiple_of(i * FFN_COL_TILE, FFN_COL_TILE), FFN_COL_TILE), :] = wd
        h = (_silu(_dot(xn_ref[...], wg)) * _dot(xn_ref[...], wu)).astype(BF16)
        acc_ref[...] += _dot(h, wd)

        @pl.when(i == n_prep - 1)
        def _():
            finish(acc_ref[...])

    @pl.when(i >= n_prep)
    def _():
        xn_ref[...] = _rms(x_ref[...], nw_ref[...]).astype(BF16)
        for c in range(n_prep):
            g = _dot(xn_ref[...], wg16_ref[c])
            u = _dot(xn_ref[...], wu16_ref[c])
            h_ref[:, c * FFN_COL_TILE:(c + 1) * FFN_COL_TILE] = (_silu(g) * u).astype(BF16)
        finish(_dot(h_ref[...], wd16_ref[...]))


def _ffn(x, norm_w, w_gate, w_up, w_down, layer, half, final_w=None):
    m, d = x.shape
    d_ff = w_gate.shape[-1]
    n_prep = d_ff // FFN_COL_TILE
    norm_row = 3 * layer + 2 * half
    row = pl.BlockSpec((FFN_ROW_TILE, d), lambda i: (jnp.maximum(i - (n_prep - 1), 0), 0))
    tile_of = lambda i: jnp.minimum(i, n_prep - 1)
    in_specs = [row,
                pl.BlockSpec((None, 1, d), lambda i: (norm_row, 0, 0)),
                pl.BlockSpec((None, None, d, FFN_COL_TILE), lambda i: (layer, half, 0, tile_of(i))),
                pl.BlockSpec((None, None, d, FFN_COL_TILE), lambda i: (layer, half, 0, tile_of(i))),
                pl.BlockSpec((None, None, FFN_COL_TILE, d), lambda i: (layer, half, tile_of(i), 0))]
    args = [x, norm_w.reshape(-1, 1, d), w_gate, w_up, w_down]
    if final_w is not None:
        in_specs.append(_resident((1, d)))
        args.append(final_w.reshape(1, d))
    return pl.pallas_call(
        functools.partial(_ffn_kernel, final_norm=final_w is not None, n_prep=n_prep),
        grid=(n_prep - 1 + m // FFN_ROW_TILE,),
        in_specs=in_specs,
        out_specs=row,
        out_shape=jax.ShapeDtypeStruct((m, d), F32),
        scratch_shapes=[pltpu.VMEM((n_prep, d, FFN_COL_TILE), BF16),
                        pltpu.VMEM((n_prep, d, FFN_COL_TILE), BF16),
                        pltpu.VMEM((d_ff, d), BF16),
                        pltpu.VMEM((FFN_ROW_TILE, d), BF16),
                        pltpu.VMEM((FFN_ROW_TILE, d_ff), BF16),
                        pltpu.VMEM((FFN_ROW_TILE, d), F32)],
        compiler_params=_params("arbitrary"),
        name="ffn",
    )(*args)


Q_A = ATTN_HEADS * HEAD_DIM
KV_A = ATTN_KV_HEADS * HEAD_DIM
QK_B = DN_HEADS * DN_DK
V_B = DN_HEADS * DN_DV
QKV_B = 2 * QK_B + V_B
MAIN_COLS = Q_A + 2 * KV_A + QKV_B + V_B


def _inproj_kernel(x_ref, nw_ref, w_ref, qa_ref, kva_ref, qkvb_ref, z_ref, gates_ref, w16_ref, wgate16_ref):
    i = pl.program_id(0)

    @pl.when(i == 0)
    def _():
        w16_ref[...] = w_ref[:, :MAIN_COLS].astype(BF16)
        wgate16_ref[...] = jnp.zeros_like(wgate16_ref)
        wgate16_ref[:, :2 * DN_HEADS] = w_ref[:, MAIN_COLS:].astype(BF16)

    @pl.when(i > 0)
    def _():
        xn = _rms(x_ref[...], nw_ref[...]).astype(BF16)
        c0, c1, c2 = Q_A, Q_A + 2 * KV_A, Q_A + 2 * KV_A + QKV_B
        qa_ref[...] = _dot(xn, w16_ref[:, :c0]).astype(BF16)
        kva_ref[...] = _dot(xn, w16_ref[:, c0:c1])
        qkvb_ref[...] = _dot(xn, w16_ref[:, c1:c2])
        z_ref[...] = _dot(xn, w16_ref[:, c2:])
        gates_ref[...] = _dot(xn, wgate16_ref[...])


def _inproj(x, norm_w, w_in, layer, e):
    m, d = x.shape
    row = lambda n: _rows_after_prep(ROW_TILE, n)
    return pl.pallas_call(
        _inproj_kernel,
        grid=(1 + m // ROW_TILE,),
        in_specs=[row(d), _layer_block((1, d), 3 * layer + 1), _layer_block(w_in.shape[1:], e)],
        out_specs=[row(Q_A), row(2 * KV_A), row(QKV_B), row(V_B), row(GATE_COLS)],
        out_shape=[jax.ShapeDtypeStruct((m, Q_A), BF16),
                   jax.ShapeDtypeStruct((m, 2 * KV_A), F32),
                   jax.ShapeDtypeStruct((m, QKV_B), F32),
                   jax.ShapeDtypeStruct((m, V_B), F32),
                   jax.ShapeDtypeStruct((m, GATE_COLS), F32)],
        scratch_shapes=[pltpu.VMEM((d, MAIN_COLS), BF16), pltpu.VMEM((d, GATE_COLS), BF16)],
        compiler_params=_params("arbitrary"),
        name="mixer_inproj",
    )(x, norm_w.reshape(-1, 1, d), w_in)


def _alibi_slopes(n_heads):
    return [float(2.0 ** (-8.0 * (h + 1) / n_heads)) for h in range(n_heads)]


ATTN_Q_TILE = 512


def _attn_kernel(sinks_ref, q_ref, kv_ref, kv_prev_ref, o_ref, *, sink_row):
    t = pl.program_id(1)
    half = lax.broadcasted_iota(jnp.int32, (1, V7X_LANES), 1) < HEAD_DIM
    kv = jnp.concatenate([kv_prev_ref[...], kv_ref[...]], axis=0)

    def placed(tile):
        swapped = pltpu.roll(tile, HEAD_DIM, axis=1)
        keep_lo = lambda x: jnp.where(half, x, 0.0).astype(BF16)
        keep_hi = lambda x: jnp.where(half, 0.0, x).astype(BF16)
        return {(0, 0): keep_lo(tile), (1, 1): keep_hi(tile), (0, 1): keep_hi(swapped), (1, 0): keep_lo(swapped)}

    k_at = placed(kv[:, :KV_A])
    v_at = placed(kv[:, KV_A:])

    i = lax.broadcasted_iota(jnp.int32, (ATTN_BLOCK, 2 * ATTN_BLOCK), 0)
    j = lax.broadcasted_iota(jnp.int32, (ATTN_BLOCK, 2 * ATTN_BLOCK), 1)
    dist = i + ATTN_BLOCK - j
    valid = jnp.where(dist >= 0, jnp.where(dist < WINDOW, 1, 0), 0) > 0
    no_prev = jnp.where(j < ATTN_BLOCK, jnp.where(t == 0, 1, 0), 0) > 0
    distf = dist.astype(F32)
    slopes = _alibi_slopes(ATTN_HEADS)
    group = ATTN_HEADS // ATTN_KV_HEADS
    bias = [jnp.where(valid, -slopes[h] * distf, -1e30) for h in range(ATTN_HEADS)]
    q = q_ref[...] * (HEAD_DIM ** -0.5)

    for qb in range(ATTN_Q_TILE // ATTN_BLOCK):
        rows = slice(qb * ATTN_BLOCK, (qb + 1) * ATTN_BLOCK)
        win = slice(qb * ATTN_BLOCK, (qb + 2) * ATTN_BLOCK)
        tiles = []
        for pair in range(ATTN_HEADS // 2):
            q_pair = q[rows, pair * V7X_LANES:(pair + 1) * V7X_LANES]
            o_pair = None
            for par in range(2):
                h = 2 * pair + par
                kh = h // group
                b_h = jnp.where(no_prev, -1e30, bias[h]) if qb == 0 else bias[h]
                s = lax.dot_general(q_pair, k_at[kh, par][win], (((1,), (1,)), ((), ())),
                                    preferred_element_type=F32) + b_h
                sink = sinks_ref[sink_row, h]
                mx = jnp.maximum(jnp.max(s, axis=-1, keepdims=True), sink)
                e = jnp.exp(s - mx)
                inv = 1.0 / (jnp.sum(e, axis=-1, keepdims=True) + jnp.exp(sink - mx))
                o_h = _dot(e.astype(BF16), v_at[kh, par][win]) * inv
                o_pair = o_h if o_pair is None else o_pair + o_h
            tiles.append(o_pair)
        o_ref[rows, :] = jnp.concatenate(tiles, axis=1).astype(BF16)


def _attention(qa, kva, sinks, e, batch, seq):
    assert KV_A == V7X_LANES and ATTN_KV_HEADS == 2 and seq % ATTN_Q_TILE == 0
    nt = seq // ATTN_Q_TILE
    prev_blocks = ATTN_Q_TILE // ATTN_BLOCK
    row_of = lambda b, t: (b * nt + t, 0)
    return pl.pallas_call(
        functools.partial(_attn_kernel, sink_row=e),
        grid=(batch, nt),
        in_specs=[pl.BlockSpec(memory_space=pltpu.SMEM),
                  pl.BlockSpec((ATTN_Q_TILE, Q_A), row_of),
                  pl.BlockSpec((ATTN_Q_TILE, 2 * KV_A), row_of),
                  pl.BlockSpec((ATTN_BLOCK, 2 * KV_A),
                               lambda b, t: (jnp.maximum((b * nt + t) * prev_blocks - 1, 0), 0))],
        out_specs=pl.BlockSpec((ATTN_Q_TILE, Q_A), row_of),
        out_shape=jax.ShapeDtypeStruct(qa.shape, BF16),
        compiler_params=_params("parallel", "parallel"),
        name="swa_attention",
    )(sinks.astype(F32), qa, kva, kva)


PAIR = V7X_LANES // DN_DK
N_PAIRS = DN_HEADS // PAIR


def _split2(x):
    hi = x.astype(BF16)
    return hi, (x - hi.astype(F32)).astype(BF16)


def _split3(x):
    hi = x.astype(BF16)
    rest = x - hi.astype(F32)
    mid = rest.astype(BF16)
    return hi, mid, (rest - mid.astype(F32)).astype(BF16)


def _select_sum(x, onehot, splitter=_split3):
    return sum(_dot(part, onehot) for part in splitter(x))


def _group_sums(x, ones_bd):
    hi, lo = _split2(x)
    cols = []
    for j in range(x.shape[1] // V7X_LANES):
        sl = slice(j * V7X_LANES, (j + 1) * V7X_LANES)
        cols.append(_dot(hi[:, sl], ones_bd) + _dot(lo[:, sl], ones_bd))
    return jnp.concatenate(cols, axis=1)


def _bdot(a, b):
    return lax.dot_general(a, b, (((2,), (1,)), ((0,), (0,))), preferred_element_type=F32)


def _bdot_nt(a, b):
    return lax.dot_general(a, b, (((2,), (2,)), ((0,), (0,))), preferred_element_type=F32)


def _blockdiag(x16, bd16):
    return jnp.concatenate([x16, x16], axis=1) * bd16


def _neumann_lower(low, bd16):
    m = -low
    m16 = m.astype(BF16)
    n = m
    m = _bdot(m16, _blockdiag(m16, bd16))
    span = 2
    while span < CHUNK:
        m16 = m.astype(BF16)
        if 2 * span >= CHUNK:
            n = n + m + _bdot(n.astype(BF16), _blockdiag(m16, bd16))
        else:
            p = _bdot(jnp.concatenate([n.astype(BF16), m16], axis=1), _blockdiag(m16, bd16))
            n = n + m + p[:, :CHUNK]
            m = p[:, CHUNK:]
        span *= 2
    return n


def _deltanet_kernel(qkv_ref, halo_ref, z_ref, gates_ref, convw_ref, alog_ref, dt_ref, normw_ref,
                     o_ref, state_ref, xpad_ref):
    t = pl.program_id(1)
    tt = qkv_ref.shape[0]
    n_chunks = tt // CHUNK

    @pl.when(t == 0)
    def _():
        state_ref[...] = jnp.zeros_like(state_ref)
        xpad_ref[0:V7X_SUBLANES, :] = jnp.zeros((V7X_SUBLANES, QKV_B), F32)

    @pl.when(t > 0)
    def _():
        xpad_ref[0:V7X_SUBLANES, :] = halo_ref[...]

    xpad_ref[V7X_SUBLANES:, :] = qkv_ref[...]
    first = V7X_SUBLANES - (DN_CONV - 1)
    acc = convw_ref[DN_CONV - 1:DN_CONV, :] * xpad_ref[pl.ds(V7X_SUBLANES, tt), :]
    for k in range(DN_CONV - 1):
        acc = acc + convw_ref[k:k + 1, :] * xpad_ref[pl.ds(first + k, tt), :]
    xc = _silu(acc)

    def iota2(shape, axis):
        return lax.broadcasted_iota(jnp.int32, shape, axis)

    sq = (V7X_LANES, V7X_LANES)
    same_head = (iota2(sq, 0) // DN_DK) == (iota2(sq, 1) // DN_DK)
    bd16 = jnp.where(same_head, 1.0, 0.0).astype(BF16)

    qk = xc[:, :2 * QK_B]
    qk = qk * lax.rsqrt(_group_sums(qk * qk, bd16) + EPS)
    q_all = qk[:, :QK_B] * (DN_DK ** -0.5)
    k_all = qk[:, QK_B:]
    v_all = xc[:, 2 * QK_B:]

    gates = gates_ref[...]
    beta_g = 1.0 / (1.0 + jnp.exp(-gates))
    g_g = -jnp.exp(alog_ref[...]) * _softplus(gates + dt_ref[...])
    same_chunk = (iota2((tt, tt), 0) // CHUNK) == (iota2((tt, tt), 1) // CHUNK)
    tril_bd = jnp.where(same_chunk, jnp.where(iota2((tt, tt), 0) >= iota2((tt, tt), 1), 1.0, 0.0),
                        0.0).astype(BF16)
    gc_g = sum(_dot(tril_bd, part) for part in _split3(g_g))
    ex = (GATE_COLS, V_B)
    spread_beta = jnp.where(iota2(ex, 0) == iota2(ex, 1) // DN_DK, 1.0, 0.0).astype(BF16)
    spread_g = jnp.where(iota2(ex, 0) == iota2(ex, 1) // DN_DK + DN_HEADS, 1.0, 0.0).astype(BF16)
    beta_x = _select_sum(beta_g, spread_beta)
    gc_x = _select_sum(gc_g, spread_g)

    egc_x = jnp.exp(gc_x)
    kb_all = k_all * beta_x
    vb_all = v_all * beta_x
    kbe_all = kb_all * egc_x
    qd_all = q_all * egc_x

    units = [(c, p) for c in range(n_chunks) for p in range(N_PAIRS)]
    packed = lambda x: jnp.stack(
        [x[c * CHUNK:(c + 1) * CHUNK, p * V7X_LANES:(p + 1) * V7X_LANES] for c, p in units], axis=0)
    gc = packed(gc_x)
    g_last = gc[:, CHUNK - 1:CHUNK, :]
    k = packed(k_all)
    k_dec_t = jnp.swapaxes(k * jnp.exp(g_last - gc), 1, 2).astype(BF16)
    state_gain = jnp.exp(g_last)

    ii = iota2((CHUNK, V7X_LANES), 0)
    jj = iota2((CHUNK, V7X_LANES), 1) % CHUNK
    causal = (ii >= jj)[None]
    strict = (ii > jj)[None]
    gc_row = jnp.sum(jnp.where((ii == jj)[None], gc, 0.0), axis=1, keepdims=True)
    decay = jnp.where(causal, jnp.exp(jnp.where(causal, gc - gc_row, 0.0)), 0.0)

    k_bd = _blockdiag(k.astype(BF16), bd16)
    kq = jnp.concatenate([packed(kb_all).astype(BF16), packed(q_all).astype(BF16)], axis=1)
    kk = _bdot_nt(kq, k_bd)
    low = jnp.where(strict, kk[:, :CHUNK] * decay, 0.0)
    attn = (kk[:, CHUNK:] * decay).astype(BF16)

    n16 = _neumann_lower(low, bd16).astype(BF16)
    vb = packed(vb_all)
    kbe = packed(kbe_all)
    rhs = jnp.concatenate([_blockdiag(vb.astype(BF16), bd16), _blockdiag(kbe.astype(BF16), bd16)], axis=2)
    uw = _bdot(n16, rhs)
    u = vb + uw[:, :, :V7X_LANES]
    w = kbe + uw[:, :, V7X_LANES:]
    state_lhs = jnp.concatenate([w.astype(BF16), packed(qd_all).astype(BF16)], axis=1)

    low_lanes = iota2((DN_DK, V7X_LANES), 1) < DN_DV
    s = state_ref[...]
    outs = []
    for c in range(n_chunks):
        g = slice(c * N_PAIRS, (c + 1) * N_PAIRS)
        ps = _bdot(state_lhs[g], _blockdiag(s.astype(BF16), bd16))
        v_new = (u[g] - ps[:, :CHUNK]).astype(BF16)
        outs.append(ps[:, CHUNK:] + _bdot(attn[g], _blockdiag(v_new, bd16)))
        upd = _bdot(k_dec_t[g], v_new)
        s = s * state_gain[g] + jnp.where(low_lanes, upd[:, :DN_DK], upd[:, DN_DK:])
    state_ref[...] = s

    o_all = jnp.concatenate(
        [jnp.concatenate([o_c[p] for p in range(N_PAIRS)], axis=1) for o_c in outs], axis=0)
    o_all = o_all * lax.rsqrt(_group_sums(o_all * o_all, bd16) * (1.0 / DN_DV) + EPS)
    o_ref[...] = (o_all * normw_ref[...] * _silu(z_ref[...])).astype(BF16)


def _deltanet(qkvb, z, gates, conv_w, a_log, dt_bias, norm_w, e, batch, seq):
    m = qkvb.shape[0]
    nt = seq // SEQ_TILE
    n_even = conv_w.shape[0]
    row_of = lambda b, t: (b * nt + t, 0)
    halo_blocks = SEQ_TILE // V7X_SUBLANES
    pad_row = lambda p: jnp.pad(p.astype(F32), ((0, 0), (DN_HEADS, GATE_COLS - 2 * DN_HEADS))).reshape(
        n_even, 1, GATE_COLS)
    return pl.pallas_call(
        _deltanet_kernel,
        grid=(batch, nt),
        in_specs=[pl.BlockSpec((SEQ_TILE, QKV_B), row_of),
                  pl.BlockSpec((V7X_SUBLANES, QKV_B),
                               lambda b, t: (jnp.maximum((b * nt + t) * halo_blocks - 1, 0), 0)),
                  pl.BlockSpec((SEQ_TILE, V_B), row_of),
                  pl.BlockSpec((SEQ_TILE, GATE_COLS), row_of),
                  _layer_block((DN_CONV, QKV_B), e),
                  _layer_block((1, GATE_COLS), e), _layer_block((1, GATE_COLS), e),
                  _layer_block((1, V_B), e)],
        out_specs=pl.BlockSpec((SEQ_TILE, V_B), row_of),
        out_shape=jax.ShapeDtypeStruct((m, V_B), BF16),
        scratch_shapes=[pltpu.VMEM((N_PAIRS, DN_DK, V7X_LANES), F32),
                        pltpu.VMEM((SEQ_TILE + V7X_SUBLANES, QKV_B), F32)],
        compiler_params=_params("parallel", "arbitrary"),
        name="gated_deltanet",
    )(qkvb, qkvb, z, gates, conv_w, pad_row(a_log), pad_row(dt_bias),
      jnp.tile(norm_w.astype(F32), (1, DN_HEADS)).reshape(n_even, 1, V_B))


def _outproj_kernel(x_ref, att_ref, dn_ref, w_ref, o_ref, w16_ref):
    i = pl.program_id(0)

    @pl.when(i == 0)
    def _():
        w16_ref[...] = w_ref[...].astype(BF16)

    @pl.when(i > 0)
    def _():
        o_ref[...] = (x_ref[...] + _dot(att_ref[...], w16_ref[:Q_A, :])
                      + _dot(dn_ref[...], w16_ref[Q_A:, :]))


def _outproj(x, att, dn, w_out, e):
    m, d = x.shape
    row = lambda n: _rows_after_prep(ROW_TILE, n)
    return pl.pallas_call(
        _outproj_kernel,
        grid=(1 + m // ROW_TILE,),
        in_specs=[row(d), row(Q_A), row(V_B), _layer_block(w_out.shape[1:], e)],
        out_specs=row(d),
        out_shape=jax.ShapeDtypeStruct((m, d), F32),
        scratch_shapes=[pltpu.VMEM(w_out.shape[1:], BF16)],
        compiler_params=_params("arbitrary"),
        name="mixer_outproj",
    )(x, att, dn, w_out)


def _glu_kernel(x_ref, nw_ref, w_ref, b_ref, o_ref, w16_ref):
    i = pl.program_id(0)

    @pl.when(i == 0)
    def _():
        w16_ref[...] = w_ref[...].astype(BF16)

    @pl.when(i > 0)
    def _():
        xn = _rms(x_ref[...], nw_ref[...]).astype(BF16)
        c = o_ref.shape[1]
        a = _dot(xn, w16_ref[:, :c]) + b_ref[:, :c]
        g = _dot(xn, w16_ref[:, c:]) + b_ref[:, c:]
        o_ref[...] = a * (1.0 / (1.0 + jnp.exp(-g)))


def _glu(x, norm_w, w, b, layer, ci):
    m, d = x.shape
    c = w.shape[2] // 2
    row = lambda n: _rows_after_prep(ROW_TILE, n)
    return pl.pallas_call(
        _glu_kernel,
        grid=(1 + m // ROW_TILE,),
        in_specs=[row(d), _layer_block((1, d), 3 * layer + 1), _layer_block(w.shape[1:], ci),
                  _layer_block((1, 2 * c), ci)],
        out_specs=row(c),
        out_shape=jax.ShapeDtypeStruct((m, c), F32),
        scratch_shapes=[pltpu.VMEM(w.shape[1:], BF16)],
        compiler_params=_params("arbitrary"),
        name="conformer_glu",
    )(x, norm_w.reshape(-1, 1, d), w, b.reshape(b.shape[0], 1, 2 * c))


CONV_HALO = 32

def _dwconv_kernel(x_ref, u_ref, halo_ref, wdw_ref, bdw_ref, lnw_ref, lnb_ref, w2_ref, b2_ref,
                   o_ref, upad_ref, shift_ref, w2_16_ref, *, width, tiles_per_seq):
    i = pl.program_id(0)

    @pl.when(i == 0)
    def _():
        w2_16_ref[...] = w2_ref[...].astype(BF16)

    @pl.when(i > 0)
    def _():
        _dwconv_tile(x_ref, u_ref, halo_ref, wdw_ref, bdw_ref, lnw_ref, lnb_ref, w2_16_ref, b2_ref,
                     o_ref, upad_ref, shift_ref, width=width, first_of_seq=(i - 1) % tiles_per_seq == 0)


def _dwconv_tile(x_ref, u_ref, halo_ref, wdw_ref, bdw_ref, lnw_ref, lnb_ref, w2_ref, b2_ref,
                 o_ref, upad_ref, shift_ref, *, width, first_of_seq):
    tt = u_ref.shape[0]

    @pl.when(first_of_seq)
    def _():
        upad_ref[0:CONV_HALO, :] = jnp.zeros((CONV_HALO, upad_ref.shape[1]), F32)

    @pl.when(jnp.logical_not(first_of_seq))
    def _():
        upad_ref[0:CONV_HALO, :] = halo_ref[...]

    upad_ref[CONV_HALO:, :] = u_ref[...]
    span = tt + CONV_HALO - V7X_SUBLANES
    for s in range(1, V7X_SUBLANES):
        shift_ref[s - 1, 0:span, :] = upad_ref[pl.ds(s, span), :]
    first = CONV_HALO - (width - 1)

    def tap_rows(k):
        tile_row, s = divmod(first + k, V7X_SUBLANES)
        start = tile_row * V7X_SUBLANES
        return upad_ref[pl.ds(start, tt), :] if s == 0 else shift_ref[s - 1, pl.ds(start, tt), :]

    acc = bdw_ref[...]
    for k in range(width):
        acc = acc + wdw_ref[k:k + 1, :] * tap_rows(k)
    mu = jnp.mean(acc, axis=-1, keepdims=True)
    xc = acc - mu
    y = xc * lax.rsqrt(jnp.mean(xc * xc, axis=-1, keepdims=True) + EPS) * lnw_ref[...] + lnb_ref[...]
    y = _silu(y).astype(BF16)
    o_ref[...] = x_ref[...] + _dot(y, w2_ref[...]) + b2_ref[...]


def _dwconv(x, u, w_dw, b_dw, ln_w, ln_b, w2, b2, ci, seq):
    m, d = x.shape
    c = u.shape[1]
    width = w_dw.shape[1]
    halo_blocks = SEQ_TILE // CONV_HALO
    vec = lambda p: p.reshape(p.shape[0], 1, p.shape[1])
    return pl.pallas_call(
        functools.partial(_dwconv_kernel, width=width, tiles_per_seq=seq // SEQ_TILE),
        grid=(1 + m // SEQ_TILE,),
        in_specs=[_rows_after_prep(SEQ_TILE, d),
                  _rows_after_prep(SEQ_TILE, c),
                  pl.BlockSpec((CONV_HALO, c), lambda i: (jnp.maximum((i - 1) * halo_blocks - 1, 0), 0)),
                  _layer_block((width, c), ci), _layer_block((1, c), ci), _layer_block((1, c), ci),
                  _layer_block((1, c), ci), _layer_block((c, d), ci), _layer_block((1, d), ci)],
        out_specs=_rows_after_prep(SEQ_TILE, d),
        out_shape=jax.ShapeDtypeStruct((m, d), F32),
        scratch_shapes=[pltpu.VMEM((SEQ_TILE + CONV_HALO, c), F32),
                        pltpu.VMEM((V7X_SUBLANES - 1, SEQ_TILE + CONV_HALO, c), F32),
                        pltpu.VMEM((c, d), BF16)],
        compiler_params=_params("arbitrary"),
        name="conformer_dwconv",
    )(x, u, u, w_dw, vec(b_dw), vec(ln_w), vec(ln_b), w2, vec(b2))


def kernel(x, norm_w, ffn_w_gate, ffn_w_up, ffn_w_down, mix_w_in, dn_conv_w, attn_sinks, dn_a_log, dn_dt_bias, dn_norm_w, mix_w_out, conv_w_pw1, conv_b_pw1, conv_w_dw, conv_b_dw, conv_ln_w, conv_ln_b, conv_w_pw2, conv_b_pw2, final_norm_w):
    batch, seq, d = x.shape
    depth = norm_w.shape[0]
    assert seq % SEQ_TILE == 0 and (batch * seq) % ROW_TILE == 0 and (batch * seq) % FFN_ROW_TILE == 0
    assert ffn_w_gate.shape[-1] % FFN_COL_TILE == 0
    assert conv_w_dw.shape[1] <= CONV_HALO + 1 and WINDOW <= ATTN_BLOCK
    assert mix_w_in.shape[2] == MAIN_COLS + 2 * DN_HEADS
    h = x.reshape(batch * seq, d)
    for layer in range(depth):
        h = _ffn(h, norm_w, ffn_w_gate, ffn_w_up, ffn_w_down, layer, 0)
        if layer % 2 == 0:
            e = layer // 2
            qa, kva, qkvb, z, gates = _inproj(h, norm_w, mix_w_in, layer, e)
            att = _attention(qa, kva, attn_sinks, e, batch, seq)
            dn = _deltanet(qkvb, z, gates, dn_conv_w, dn_a_log, dn_dt_bias, dn_norm_w, e, batch, seq)
            h = _outproj(h, att, dn, mix_w_out, e)
        else:
            c = layer // 2
            u = _glu(h, norm_w, conv_w_pw1, conv_b_pw1, layer, c)
            h = _dwconv(h, u, conv_w_dw, conv_b_dw, conv_ln_w, conv_ln_b, conv_w_pw2, conv_b_pw2, c, seq)
        last = layer == depth - 1
        h = _ffn(h, norm_w, ffn_w_gate, ffn_w_up, ffn_w_down, layer, 1,
                 final_w=final_norm_w if last else None)
    return h.reshape(batch, seq, d)
```

```python
import functools

import jax
import jax.numpy as jnp
from jax import lax
from jax.experimental import pallas as pl
from jax.experimental.pallas import tpu as pltpu

F32 = jnp.float32
BF16 = jnp.bfloat16

ATTN_HEADS = 8
ATTN_KV_HEADS = 2
HEAD_DIM = 64
WINDOW = 128
ATTN_BLOCK = 128
DN_HEADS = 8
DN_DK = 64
DN_DV = 64
DN_CONV = 4
CHUNK = 64
EPS = 1e-6

V7X_LANES = 128
V7X_SUBLANES = 8
V7X_VMEM_LIMIT_BYTES = 56 * 1024 * 1024

ROW_TILE = 1024
FFN_ROW_TILE = 1024
FFN_COL_TILE = 256
SEQ_TILE = 512
DN_SEQ_TILE = 128
DN_SEQS = 4
GATE_COLS = V7X_LANES


def _params(*sem):
    return pltpu.CompilerParams(dimension_semantics=sem, vmem_limit_bytes=V7X_VMEM_LIMIT_BYTES)


def _rms(x, w):
    return x * lax.rsqrt(jnp.mean(x * x, axis=-1, keepdims=True) + EPS) * w


def _silu(x):
    return x * (1.0 / (1.0 + jnp.exp(-x)))


def _softplus(x):
    return jnp.maximum(x, 0.0) + jnp.log(1.0 + jnp.exp(-jnp.abs(x)))


def _dot(a, b):
    return jnp.dot(a, b, preferred_element_type=F32)


def _resident(shape):
    return pl.BlockSpec(shape, lambda *_: (0,) * len(shape))


def _layer_block(shape, index):
    return pl.BlockSpec((None,) + tuple(shape), lambda *_: (index,) + (0,) * len(shape),
                        pipeline_mode=pl.Buffered(1))


def _rows_after_prep(n_rows, n_cols):
    return pl.BlockSpec((n_rows, n_cols), lambda i: (jnp.maximum(i - 1, 0), 0))


def _ffn_kernel(x_ref, nw_ref, wg_ref, wu_ref, wd_ref, *rest, final_norm, n_prep):
    if final_norm:
        fw_ref, o_ref, wg16_ref, wu16_ref, wd16_ref, xn_ref, h_ref, acc_ref = rest
    else:
        o_ref, wg16_ref, wu16_ref, wd16_ref, xn_ref, h_ref, acc_ref = rest
    i = pl.program_id(0)

    def finish(ffn_out):
        y = x_ref[...] + 0.5 * ffn_out
        if final_norm:
            y = _rms(y, fw_ref[...])
        o_ref[...] = y

    @pl.when(i < n_prep)
    def _():
        @pl.when(i == 0)
        def _():
            xn_ref[...] = _rms(x_ref[...], nw_ref[...]).astype(BF16)
            acc_ref[...] = jnp.zeros_like(acc_ref)

        wg, wu, wd = wg_ref[...].astype(BF16), wu_ref[...].astype(BF16), wd_ref[...].astype(BF16)
        wg16_ref[i] = wg
        wu16_ref[i] = wu
        wd16_ref[pl.ds(pl.multiple_of(i * FFN_COL_TILE, FFN_COL_TILE), FFN_COL_TILE), :] = wd
        h = (_silu(_dot(xn_ref[...], wg)) * _dot(xn_ref[...], wu)).astype(BF16)
        acc_ref[...] += _dot(h, wd)

        @pl.when(i == n_prep - 1)
        def _():
            finish(acc_ref[...])

    @pl.when(i >= n_prep)
    def _():
        xn_ref[...] = _rms(x_ref[...], nw_ref[...]).astype(BF16)
        for c in range(n_prep):
            g = _dot(xn_ref[...], wg16_ref[c])
            u = _dot(xn_ref[...], wu16_ref[c])
            h_ref[:, c * FFN_COL_TILE:(c + 1) * FFN_COL_TILE] = (_silu(g) * u).astype(BF16)
        finish(_dot(h_ref[...], wd16_ref[...]))


def _ffn(x, norm_w, w_gate, w_up, w_down, layer, half, final_w=None):
    m, d = x.shape
    d_ff = w_gate.shape[-1]
    n_prep = d_ff // FFN_COL_TILE
    norm_row = 3 * layer + 2 * half
    row = pl.BlockSpec((FFN_ROW_TILE, d), lambda i: (jnp.maximum(i - (n_prep - 1), 0), 0))
    tile_of = lambda i: jnp.minimum(i, n_prep - 1)
    in_specs = [row,
                pl.BlockSpec((None, 1, d), lambda i: (norm_row, 0, 0)),
                pl.BlockSpec((None, None, d, FFN_COL_TILE), lambda i: (layer, half, 0, tile_of(i))),
                pl.BlockSpec((None, None, d, FFN_COL_TILE), lambda i: (layer, half, 0, tile_of(i))),
                pl.BlockSpec((None, None, FFN_COL_TILE, d), lambda i: (layer, half, tile_of(i), 0))]
    args = [x, norm_w.reshape(-1, 1, d), w_gate, w_up, w_down]
    if final_w is not None:
        in_specs.append(_resident((1, d)))
        args.append(final_w.reshape(1, d))
    return pl.pallas_call(
        functools.partial(_ffn_kernel, final_norm=final_w is not None, n_prep=n_prep),
        grid=(n_prep - 1 + m // FFN_ROW_TILE,),
        in_specs=in_specs,
        out_specs=row,
        out_shape=jax.ShapeDtypeStruct((m, d), F32),
        scratch_shapes=[pltpu.VMEM((n_prep, d, FFN_COL_TILE), BF16),
                        pltpu.VMEM((n_prep, d, FFN_COL_TILE), BF16),
                        pltpu.VMEM((d_ff, d), BF16),
                        pltpu.VMEM((FFN_ROW_TILE, d), BF16),
                        pltpu.VMEM((FFN_ROW_TILE, d_ff), BF16),
                        pltpu.VMEM((FFN_ROW_TILE, d), F32)],
        compiler_params=_params("arbitrary"),
        name="ffn",
    )(*args)


Q_A = ATTN_HEADS * HEAD_DIM
KV_A = ATTN_KV_HEADS * HEAD_DIM
QK_B = DN_HEADS * DN_DK
V_B = DN_HEADS * DN_DV
QKV_B = 2 * QK_B + V_B
MAIN_COLS = Q_A + 2 * KV_A + QKV_B + V_B


def _inproj_kernel(x_ref, nw_ref, w_ref, qa_ref, kva_ref, qkvb_ref, z_ref, gates_ref, w16_ref, wgate16_ref):
    i = pl.program_id(0)

    @pl.when(i == 0)
    def _():
        w16_ref[...] = w_ref[:, :MAIN_COLS].astype(BF16)
        wgate16_ref[...] = jnp.zeros_like(wgate16_ref)
        wgate16_ref[:, :2 * DN_HEADS] = w_ref[:, MAIN_COLS:].astype(BF16)

    @pl.when(i > 0)
    def _():
        xn = _rms(x_ref[...], nw_ref[...]).astype(BF16)
        c0, c1, c2 = Q_A, Q_A + 2 * KV_A, Q_A + 2 * KV_A + QKV_B
        qa_ref[...] = _dot(xn, w16_ref[:, :c0]).astype(BF16)
        kva_ref[...] = _dot(xn, w16_ref[:, c0:c1])
        qkvb_ref[...] = _dot(xn, w16_ref[:, c1:c2])
        z_ref[...] = _dot(xn, w16_ref[:, c2:])
        gates_ref[...] = _dot(xn, wgate16_ref[...])


def _inproj(x, norm_w, w_in, layer, e):
    m, d = x.shape
    row = lambda n: _rows_after_prep(ROW_TILE, n)
    return pl.pallas_call(
        _inproj_kernel,
        grid=(1 + m // ROW_TILE,),
        in_specs=[row(d), _layer_block((1, d), 3 * layer + 1), _layer_block(w_in.shape[1:], e)],
        out_specs=[row(Q_A), row(2 * KV_A), row(QKV_B), row(V_B), row(GATE_COLS)],
        out_shape=[jax.ShapeDtypeStruct((m, Q_A), BF16),
                   jax.ShapeDtypeStruct((m, 2 * KV_A), F32),
                   jax.ShapeDtypeStruct((m, QKV_B), F32),
                   jax.ShapeDtypeStruct((m, V_B), F32),
                   jax.ShapeDtypeStruct((m, GATE_COLS), F32)],
        scratch_shapes=[pltpu.VMEM((d, MAIN_COLS), BF16), pltpu.VMEM((d, GATE_COLS), BF16)],
        compiler_params=_params("arbitrary"),
        name="mixer_inproj",
    )(x, norm_w.reshape(-1, 1, d), w_in)


def _alibi_slopes(n_heads):
    return [float(2.0 ** (-8.0 * (h + 1) / n_heads)) for h in range(n_heads)]


ATTN_Q_TILE = 512


def _attn_kernel(sinks_ref, q_ref, kv_ref, kv_prev_ref, o_ref, *, sink_row):
    t = pl.program_id(1)
    half = lax.broadcasted_iota(jnp.int32, (1, V7X_LANES), 1) < HEAD_DIM
    kv = jnp.concatenate([kv_prev_ref[...], kv_ref[...]], axis=0)

    def placed(tile):
        swapped = pltpu.roll(tile, HEAD_DIM, axis=1)
        keep_lo = lambda x: jnp.where(half, x, 0.0).astype(BF16)
        keep_hi = lambda x: jnp.where(half, 0.0, x).astype(BF16)
        return {(0, 0): keep_lo(tile), (1, 1): keep_hi(tile), (0, 1): keep_hi(swapped), (1, 0): keep_lo(swapped)}

    k_at = placed(kv[:, :KV_A])
    v_at = placed(kv[:, KV_A:])

    i = lax.broadcasted_iota(jnp.int32, (ATTN_BLOCK, 2 * ATTN_BLOCK), 0)
    j = lax.broadcasted_iota(jnp.int32, (ATTN_BLOCK, 2 * ATTN_BLOCK), 1)
    dist = i + ATTN_BLOCK - j
    valid = jnp.where(dist >= 0, jnp.where(dist < WINDOW, 1, 0), 0) > 0
    no_prev = jnp.where(j < ATTN_BLOCK, jnp.where(t == 0, 1, 0), 0) > 0
    distf = dist.astype(F32)
    slopes = _alibi_slopes(ATTN_HEADS)
    group = ATTN_HEADS // ATTN_KV_HEADS
    bias = [jnp.where(valid, -slopes[h] * distf, -1e30) for h in range(ATTN_HEADS)]
    q = q_ref[...] * (HEAD_DIM ** -0.5)

    for qb in range(ATTN_Q_TILE // ATTN_BLOCK):
        rows = slice(qb * ATTN_BLOCK, (qb + 1) * ATTN_BLOCK)
        win = slice(qb * ATTN_BLOCK, (qb + 2) * ATTN_BLOCK)
        tiles = []
        for pair in range(ATTN_HEADS // 2):
            q_pair = q[rows, pair * V7X_LANES:(pair + 1) * V7X_LANES]
            o_pair = None
            for par in range(2):
                h = 2 * pair + par
                kh = h // group
                b_h = jnp.where(no_prev, -1e30, bias[h]) if qb == 0 else bias[h]
                s = lax.dot_general(q_pair, k_at[kh, par][win], (((1,), (1,)), ((), ())),
                                    preferred_element_type=F32) + b_h
                sink = sinks_ref[sink_row, h]
                mx = jnp.maximum(jnp.max(s, axis=-1, keepdims=True), sink)
                e = jnp.exp(s - mx)
                inv = 1.0 / (jnp.sum(e, axis=-1, keepdims=True) + jnp.exp(sink - mx))
                o_h = _dot(e.astype(BF16), v_at[kh, par][win]) * inv
                o_pair = o_h if o_pair is None else o_pair + o_h
            tiles.append(o_pair)
        o_ref[rows, :] = jnp.concatenate(tiles, axis=1).astype(BF16)


def _attention(qa, kva, sinks, e, batch, seq):
    assert KV_A == V7X_LANES and ATTN_KV_HEADS == 2 and seq % ATTN_Q_TILE == 0
    nt = seq // ATTN_Q_TILE
    prev_blocks = ATTN_Q_TILE // ATTN_BLOCK
    row_of = lambda b, t: (b * nt + t, 0)
    return pl.pallas_call(
        functools.partial(_attn_kernel, sink_row=e),
        grid=(batch, nt),
        in_specs=[pl.BlockSpec(memory_space=pltpu.SMEM),
                  pl.BlockSpec((ATTN_Q_TILE, Q_A), row_of),
                  pl.BlockSpec((ATTN_Q_TILE, 2 * KV_A), row_of),
                  pl.BlockSpec((ATTN_BLOCK, 2 * KV_A),
                               lambda b, t: (jnp.maximum((b * nt + t) * prev_blocks - 1, 0), 0))],
        out_specs=pl.BlockSpec((ATTN_Q_TILE, Q_A), row_of),
        out_shape=jax.ShapeDtypeStruct(qa.shape, BF16),
        compiler_params=_params("parallel", "parallel"),
        name="swa_attention",
    )(sinks.astype(F32), qa, kva, kva)


PAIR = V7X_LANES // DN_DK
N_PAIRS = DN_HEADS // PAIR


def _split2(x):
    hi = x.astype(BF16)
    return hi, (x - hi.astype(F32)).astype(BF16)


def _split3(x):
    hi = x.astype(BF16)
    rest = x - hi.astype(F32)
    mid = rest.astype(BF16)
    return hi, mid, (rest - mid.astype(F32)).astype(BF16)


def _select_sum(x, onehot, splitter=_split3):
    return sum(_dot(part, onehot) for part in splitter(x))


def _group_sums(x, ones_bd):
    hi, lo = _split2(x)
    cols = []
    for j in range(x.shape[1] // V7X_LANES):
        sl = slice(j * V7X_LANES, (j + 1) * V7X_LANES)
        cols.append(_dot(hi[:, sl], ones_bd) + _dot(lo[:, sl], ones_bd))
    return jnp.concatenate(cols, axis=1)


def _bdot(a, b):
    return lax.dot_general(a, b, (((2,), (1,)), ((0,), (0,))), preferred_element_type=F32)


def _bdot_nt(a, b):
    return lax.dot_general(a, b, (((2,), (2,)), ((0,), (0,))), preferred_element_type=F32)


def _blockdiag(x, same_head):
    return jnp.where(same_head, jnp.concatenate([x, x], axis=1), 0.0).astype(BF16)


def _neumann_lower(low, same_head):
    m = -low
    m16 = m.astype(BF16)
    n = m
    m = _bdot(m16, _blockdiag(m, same_head))
    span = 2
    while span < CHUNK:
        m16 = m.astype(BF16)
        if 2 * span >= CHUNK:
            n = n + m + _bdot(n.astype(BF16), _blockdiag(m, same_head))
        else:
            p = _bdot(jnp.concatenate([n.astype(BF16), m16], axis=1), _blockdiag(m, same_head))
            n = n + m + p[:, :CHUNK]
            m = p[:, CHUNK:]
        span *= 2
    return n


def _deltanet_kernel(qkv_ref, halo_ref, z_ref, gates_ref, convw_ref, alog_ref, dt_ref, normw_ref,
                     o_ref, state_ref, xpad_ref):
    t = pl.program_id(1)
    n_seqs, tt = qkv_ref.shape[0], qkv_ref.shape[1]
    rows = n_seqs * tt
    chunks_per_seq = tt // CHUNK
    n_chunks = n_seqs * chunks_per_seq

    @pl.when(t == 0)
    def _():
        state_ref[...] = jnp.zeros_like(state_ref)
        xpad_ref[:, 0:V7X_SUBLANES, :] = jnp.zeros((n_seqs, V7X_SUBLANES, QKV_B), F32)

    @pl.when(t > 0)
    def _():
        xpad_ref[:, 0:V7X_SUBLANES, :] = halo_ref[...]

    xpad_ref[:, V7X_SUBLANES:, :] = qkv_ref[...]
    first = V7X_SUBLANES - (DN_CONV - 1)
    acc = convw_ref[DN_CONV - 1:DN_CONV, :] * xpad_ref[:, pl.ds(V7X_SUBLANES, tt), :]
    for k in range(DN_CONV - 1):
        acc = acc + convw_ref[k:k + 1, :] * xpad_ref[:, pl.ds(first + k, tt), :]
    xc = _silu(acc).reshape(rows, QKV_B)

    def iota2(shape, axis):
        return lax.broadcasted_iota(jnp.int32, shape, axis)

    sq = (V7X_LANES, V7X_LANES)
    same_head = (iota2(sq, 0) // DN_DK) == (iota2(sq, 1) // DN_DK)
    bd16 = jnp.where(same_head, 1.0, 0.0).astype(BF16)

    qk = xc[:, :2 * QK_B]
    qk = qk * lax.rsqrt(_group_sums(qk * qk, bd16) + EPS)
    q_all = qk[:, :QK_B] * (DN_DK ** -0.5)
    k_all = qk[:, QK_B:]
    v_all = xc[:, 2 * QK_B:]

    gates = gates_ref[...].reshape(rows, GATE_COLS)
    beta_g = 1.0 / (1.0 + jnp.exp(-gates))
    g_g = -jnp.exp(alog_ref[...]) * _softplus(gates + dt_ref[...])
    tril = jnp.where(iota2((CHUNK, CHUNK), 0) >= iota2((CHUNK, CHUNK), 1), 1.0, 0.0).astype(BF16)
    g_parts = _split3(g_g)
    gc_g = jnp.concatenate(
        [sum(_dot(tril, part[c * CHUNK:(c + 1) * CHUNK]) for part in g_parts) for c in range(n_chunks)], axis=0)
    ex = (GATE_COLS, V_B)
    spread_beta = jnp.where(iota2(ex, 0) == iota2(ex, 1) // DN_DK, 1.0, 0.0).astype(BF16)
    spread_g = jnp.where(iota2(ex, 0) == iota2(ex, 1) // DN_DK + DN_HEADS, 1.0, 0.0).astype(BF16)
    beta_x = _select_sum(beta_g, spread_beta)
    gc_x = _select_sum(gc_g, spread_g)

    egc_x = jnp.exp(gc_x)
    kb_all = k_all * beta_x
    vb_all = v_all * beta_x
    kbe_all = kb_all * egc_x
    qd_all = q_all * egc_x

    units = [(c, p) for c in range(n_chunks) for p in range(N_PAIRS)]
    packed = lambda x: jnp.stack(
        [x[c * CHUNK:(c + 1) * CHUNK, p * V7X_LANES:(p + 1) * V7X_LANES] for c, p in units], axis=0)
    gc = packed(gc_x)
    g_last = gc[:, CHUNK - 1:CHUNK, :]
    k = packed(k_all)
    k_dec_t = jnp.swapaxes(k * jnp.exp(g_last - gc), 1, 2).astype(BF16)
    state_gain = jnp.exp(g_last)

    ii = iota2((CHUNK, V7X_LANES), 0)
    jj = iota2((CHUNK, V7X_LANES), 1) % CHUNK
    causal = (ii >= jj)[None]
    strict = (ii > jj)[None]
    gc_row = jnp.sum(jnp.where((ii == jj)[None], gc, 0.0), axis=1, keepdims=True)
    decay = jnp.where(causal, jnp.exp(jnp.where(causal, gc - gc_row, 0.0)), 0.0)

    k_bd = _blockdiag(k, same_head)
    kq = jnp.concatenate([packed(kb_all).astype(BF16), packed(q_all).astype(BF16)], axis=1)
    kk = _bdot_nt(kq, k_bd)
    low = jnp.where(strict, kk[:, :CHUNK] * decay, 0.0)
    attn = (kk[:, CHUNK:] * decay).astype(BF16)

    n16 = _neumann_lower(low, same_head).astype(BF16)
    vb = packed(vb_all)
    kbe = packed(kbe_all)
    rhs = jnp.concatenate([_blockdiag(vb, same_head), _blockdiag(kbe, same_head)], axis=2)
    uw = _bdot(n16, rhs)
    u = vb + uw[:, :, :V7X_LANES]
    w = kbe + uw[:, :, V7X_LANES:]
    state_lhs = jnp.concatenate([w.astype(BF16), packed(qd_all).astype(BF16)], axis=1)

    low_lanes = iota2((DN_DK, V7X_LANES), 1) < DN_DV
    s = state_ref[...]
    outs = []
    for c in range(chunks_per_seq):
        at_c = lambda x: jnp.concatenate(
            [x[(sq * chunks_per_seq + c) * N_PAIRS:(sq * chunks_per_seq + c + 1) * N_PAIRS]
             for sq in range(n_seqs)], axis=0)
        ps = _bdot(at_c(state_lhs), _blockdiag(s, same_head))
        v_new = at_c(u) - ps[:, :CHUNK]
        outs.append(ps[:, CHUNK:] + _bdot(at_c(attn), _blockdiag(v_new, same_head)))
        upd = _bdot(at_c(k_dec_t), v_new.astype(BF16))
        s = s * at_c(state_gain) + jnp.where(low_lanes, upd[:, :DN_DK], upd[:, DN_DK:])
    state_ref[...] = s

    o_all = jnp.concatenate(
        [jnp.concatenate([outs[c][sq * N_PAIRS + p] for p in range(N_PAIRS)], axis=1)
         for sq in range(n_seqs) for c in range(chunks_per_seq)], axis=0)
    o_all = o_all * lax.rsqrt(_group_sums(o_all * o_all, bd16) * (1.0 / DN_DV) + EPS)
    gated = o_all * normw_ref[...] * _silu(z_ref[...].reshape(rows, V_B))
    o_ref[...] = gated.reshape(n_seqs, tt, V_B).astype(BF16)


def _deltanet(qkvb, z, gates, conv_w, a_log, dt_bias, norm_w, e, batch, seq):
    m = qkvb.shape[0]
    assert batch % DN_SEQS == 0 and seq % DN_SEQ_TILE == 0
    n_even = conv_w.shape[0]
    halo_blocks = DN_SEQ_TILE // V7X_SUBLANES
    per_seq = lambda a: a.reshape(batch, seq, a.shape[-1])
    tile = lambda n: pl.BlockSpec((DN_SEQS, DN_SEQ_TILE, n), lambda b, t: (b, t, 0))
    pad_row = lambda p: jnp.pad(p.astype(F32), ((0, 0), (DN_HEADS, GATE_COLS - 2 * DN_HEADS))).reshape(
        n_even, 1, GATE_COLS)
    out = pl.pallas_call(
        _deltanet_kernel,
        grid=(batch // DN_SEQS, seq // DN_SEQ_TILE),
        in_specs=[tile(QKV_B),
                  pl.BlockSpec((DN_SEQS, V7X_SUBLANES, QKV_B),
                               lambda b, t: (b, jnp.maximum(t * halo_blocks - 1, 0), 0)),
                  tile(V_B),
                  tile(GATE_COLS),
                  _layer_block((DN_CONV, QKV_B), e),
                  _layer_block((1, GATE_COLS), e), _layer_block((1, GATE_COLS), e),
                  _layer_block((1, V_B), e)],
        out_specs=tile(V_B),
        out_shape=jax.ShapeDtypeStruct((batch, seq, V_B), BF16),
        scratch_shapes=[pltpu.VMEM((DN_SEQS * N_PAIRS, DN_DK, V7X_LANES), F32),
                        pltpu.VMEM((DN_SEQS, DN_SEQ_TILE + V7X_SUBLANES, QKV_B), F32)],
        compiler_params=_params("parallel", "arbitrary"),
        name="gated_deltanet",
    )(per_seq(qkvb), per_seq(qkvb), per_seq(z), per_seq(gates), conv_w, pad_row(a_log), pad_row(dt_bias),
      jnp.tile(norm_w.astype(F32), (1, DN_HEADS)).reshape(n_even, 1, V_B))
    return out.reshape(m, V_B)


def _outproj_kernel(x_ref, att_ref, dn_ref, w_ref, o_ref, w16_ref):
    i = pl.program_id(0)

    @pl.when(i == 0)
    def _():
        w16_ref[...] = w_ref[...].astype(BF16)

    @pl.when(i > 0)
    def _():
        o_ref[...] = (x_ref[...] + _dot(att_ref[...], w16_ref[:Q_A, :])
                      + _dot(dn_ref[...], w16_ref[Q_A:, :]))


def _outproj(x, att, dn, w_out, e):
    m, d = x.shape
    row = lambda n: _rows_after_prep(ROW_TILE, n)
    return pl.pallas_call(
        _outproj_kernel,
        grid=(1 + m // ROW_TILE,),
        in_specs=[row(d), row(Q_A), row(V_B), _layer_block(w_out.shape[1:], e)],
        out_specs=row(d),
        out_shape=jax.ShapeDtypeStruct((m, d), F32),
        scratch_shapes=[pltpu.VMEM(w_out.shape[1:], BF16)],
        compiler_params=_params("arbitrary"),
        name="mixer_outproj",
    )(x, att, dn, w_out)


def _glu_kernel(x_ref, nw_ref, w_ref, b_ref, o_ref, w16_ref):
    i = pl.program_id(0)

    @pl.when(i == 0)
    def _():
        w16_ref[...] = w_ref[...].astype(BF16)

    @pl.when(i > 0)
    def _():
        xn = _rms(x_ref[...], nw_ref[...]).astype(BF16)
        c = o_ref.shape[1]
        a = _dot(xn, w16_ref[:, :c]) + b_ref[:, :c]
        g = _dot(xn, w16_ref[:, c:]) + b_ref[:, c:]
        o_ref[...] = a * (1.0 / (1.0 + jnp.exp(-g)))


def _glu(x, norm_w, w, b, layer, ci):
    m, d = x.shape
    c = w.shape[2] // 2
    row = lambda n: _rows_after_prep(ROW_TILE, n)
    return pl.pallas_call(
        _glu_kernel,
        grid=(1 + m // ROW_TILE,),
        in_specs=[row(d), _layer_block((1, d), 3 * layer + 1), _layer_block(w.shape[1:], ci),
                  _layer_block((1, 2 * c), ci)],
        out_specs=row(c),
        out_shape=jax.ShapeDtypeStruct((m, c), F32),
        scratch_shapes=[pltpu.VMEM(w.shape[1:], BF16)],
        compiler_params=_params("arbitrary"),
        name="conformer_glu",
    )(x, norm_w.reshape(-1, 1, d), w, b.reshape(b.shape[0], 1, 2 * c))


CONV_HALO = 32

def _dwconv_kernel(x_ref, u_ref, halo_ref, wdw_ref, bdw_ref, lnw_ref, lnb_ref, w2_ref, b2_ref,
                   o_ref, upad_ref, shift_ref, w2_16_ref, *, width, tiles_per_seq):
    i = pl.program_id(0)

    @pl.when(i == 0)
    def _():
        w2_16_ref[...] = w2_ref[...].astype(BF16)

    @pl.when(i > 0)
    def _():
        _dwconv_tile(x_ref, u_ref, halo_ref, wdw_ref, bdw_ref, lnw_ref, lnb_ref, w2_16_ref, b2_ref,
                     o_ref, upad_ref, shift_ref, width=width, first_of_seq=(i - 1) % tiles_per_seq == 0)


def _dwconv_tile(x_ref, u_ref, halo_ref, wdw_ref, bdw_ref, lnw_ref, lnb_ref, w2_ref, b2_ref,
                 o_ref, upad_ref, shift_ref, *, width, first_of_seq):
    tt = u_ref.shape[0]

    @pl.when(first_of_seq)
    def _():
        upad_ref[0:CONV_HALO, :] = jnp.zeros((CONV_HALO, upad_ref.shape[1]), F32)

    @pl.when(jnp.logical_not(first_of_seq))
    def _():
        upad_ref[0:CONV_HALO, :] = halo_ref[...]

    upad_ref[CONV_HALO:, :] = u_ref[...]
    span = tt + CONV_HALO - V7X_SUBLANES
    for s in range(1, V7X_SUBLANES):
        shift_ref[s - 1, 0:span, :] = upad_ref[pl.ds(s, span), :]
    first = CONV_HALO - (width - 1)

    def tap_rows(k):
        tile_row, s = divmod(first + k, V7X_SUBLANES)
        start = tile_row * V7X_SUBLANES
        return upad_ref[pl.ds(start, tt), :] if s == 0 else shift_ref[s - 1, pl.ds(start, tt), :]

    acc = bdw_ref[...]
    for k in range(width):
        acc = acc + wdw_ref[k:k + 1, :] * tap_rows(k)
    mu = jnp.mean(acc, axis=-1, keepdims=True)
    xc = acc - mu
    y = xc * lax.rsqrt(jnp.mean(xc * xc, axis=-1, keepdims=True) + EPS) * lnw_ref[...] + lnb_ref[...]
    y = _silu(y).astype(BF16)
    o_ref[...] = x_ref[...] + _dot(y, w2_ref[...]) + b2_ref[...]


def _dwconv(x, u, w_dw, b_dw, ln_w, ln_b, w2, b2, ci, seq):
    m, d = x.shape
    c = u.shape[1]
    width = w_dw.shape[1]
    halo_blocks = SEQ_TILE // CONV_HALO
    vec = lambda p: p.reshape(p.shape[0], 1, p.shape[1])
    return pl.pallas_call(
        functools.partial(_dwconv_kernel, width=width, tiles_per_seq=seq // SEQ_TILE),
        grid=(1 + m // SEQ_TILE,),
        in_specs=[_rows_after_prep(SEQ_TILE, d),
                  _rows_after_prep(SEQ_TILE, c),
                  pl.BlockSpec((CONV_HALO, c), lambda i: (jnp.maximum((i - 1) * halo_blocks - 1, 0), 0)),
                  _layer_block((width, c), ci), _layer_block((1, c), ci), _layer_block((1, c), ci),
                  _layer_block((1, c), ci), _layer_block((c, d), ci), _layer_block((1, d), ci)],
        out_specs=_rows_after_prep(SEQ_TILE, d),
        out_shape=jax.ShapeDtypeStruct((m, d), F32),
        scratch_shapes=[pltpu.VMEM((SEQ_TILE + CONV_HALO, c), F32),
                        pltpu.VMEM((V7X_SUBLANES - 1, SEQ_TILE + CONV_HALO, c), F32),
                        pltpu.VMEM((c, d), BF16)],
        compiler_params=_params("arbitrary"),
        name="conformer_dwconv",
    )(x, u, u, w_dw, vec(b_dw), vec(ln_w), vec(ln_b), w2, vec(b2))


def kernel(x, norm_w, ffn_w_gate, ffn_w_up, ffn_w_down, mix_w_in, dn_conv_w, attn_sinks, dn_a_log, dn_dt_bias, dn_norm_w, mix_w_out, conv_w_pw1, conv_b_pw1, conv_w_dw, conv_b_dw, conv_ln_w, conv_ln_b, conv_w_pw2, conv_b_pw2, final_norm_w):
    batch, seq, d = x.shape
    depth = norm_w.shape[0]
    assert seq % SEQ_TILE == 0 and (batch * seq) % ROW_TILE == 0 and (batch * seq) % FFN_ROW_TILE == 0
    assert ffn_w_gate.shape[-1] % FFN_COL_TILE == 0
    assert conv_w_dw.shape[1] <= CONV_HALO + 1 and WINDOW <= ATTN_BLOCK
    assert mix_w_in.shape[2] == MAIN_COLS + 2 * DN_HEADS
    h = x.reshape(batch * seq, d)
    for layer in range(depth):
        h = _ffn(h, norm_w, ffn_w_gate, ffn_w_up, ffn_w_down, layer, 0)
        if layer % 2 == 0:
            e = layer // 2
            qa, kva, qkvb, z, gates = _inproj(h, norm_w, mix_w_in, layer, e)
            att = _attention(qa, kva, attn_sinks, e, batch, seq)
            dn = _deltanet(qkvb, z, gates, dn_conv_w, dn_a_log, dn_dt_bias, dn_norm_w, e, batch, seq)
            h = _outproj(h, att, dn, mix_w_out, e)
        else:
            c = layer // 2
            u = _glu(h, norm_w, conv_w_pw1, conv_b_pw1, layer, c)
            h = _dwconv(h, u, conv_w_dw, conv_b_dw, conv_ln_w, conv_ln_b, conv_w_pw2, conv_b_pw2, c, seq)
        last = layer == depth - 1
        h = _ffn(h, norm_w, ffn_w_gate, ffn_w_up, ffn_w_down, layer, 1,
                 final_w=final_norm_w if last else None)
    return h.reshape(batch, seq, d)
```

```python
import functools

import jax
import jax.numpy as jnp
from jax import lax
from jax.experimental import pallas as pl
from jax.experimental.pallas import tpu as pltpu

F32 = jnp.float32
BF16 = jnp.bfloat16

ATTN_HEADS = 8
ATTN_KV_HEADS = 2
HEAD_DIM = 64
WINDOW = 128
ATTN_BLOCK = 128
DN_HEADS = 8
DN_DK = 64
DN_DV = 64
DN_CONV = 4
CHUNK = 64
EPS = 1e-6

V7X_LANES = 128
V7X_SUBLANES = 8
V7X_VMEM_LIMIT_BYTES = 56 * 1024 * 1024

ROW_TILE = 1024
FFN_ROW_TILE = 1024
FFN_COL_TILE = 256
SEQ_TILE = 512
DN_SEQ_TILE = 128
DN_SEQS = 4
GATE_COLS = V7X_LANES
GLU_COL_TILE = 256


def _params(*sem):
    return pltpu.CompilerParams(dimension_semantics=sem, vmem_limit_bytes=V7X_VMEM_LIMIT_BYTES)


def _rms(x, w):
    return x * lax.rsqrt(jnp.mean(x * x, axis=-1, keepdims=True) + EPS) * w


def _silu(x):
    return x * (1.0 / (1.0 + jnp.exp(-x)))


def _softplus(x):
    return jnp.maximum(x, 0.0) + jnp.log(1.0 + jnp.exp(-jnp.abs(x)))


def _dot(a, b):
    return jnp.dot(a, b, preferred_element_type=F32)


def _resident(shape):
    return pl.BlockSpec(shape, lambda *_: (0,) * len(shape))


def _layer_block(shape, index):
    return pl.BlockSpec((None,) + tuple(shape), lambda *_: (index,) + (0,) * len(shape),
                        pipeline_mode=pl.Buffered(1))


def _rows_after_prep(n_rows, n_cols):
    return pl.BlockSpec((n_rows, n_cols), lambda i: (jnp.maximum(i - 1, 0), 0))


def _ffn_kernel(x_ref, nw_ref, wg_ref, wu_ref, wd_ref, *rest, final_norm, n_prep):
    if final_norm:
        fw_ref, o_ref, wg16_ref, wu16_ref, wd16_ref, xn_ref, h_ref, acc_ref = rest
    else:
        o_ref, wg16_ref, wu16_ref, wd16_ref, xn_ref, h_ref, acc_ref = rest
    i = pl.program_id(0)

    def finish(ffn_out):
        y = x_ref[...] + 0.5 * ffn_out
        if final_norm:
            y = _rms(y, fw_ref[...])
        o_ref[...] = y

    @pl.when(i < n_prep)
    def _():
        @pl.when(i == 0)
        def _():
            xn_ref[...] = _rms(x_ref[...], nw_ref[...]).astype(BF16)
            acc_ref[...] = jnp.zeros_like(acc_ref)

        wg, wu, wd = wg_ref[...].astype(BF16), wu_ref[...].astype(BF16), wd_ref[...].astype(BF16)
        wg16_ref[i] = wg
        wu16_ref[i] = wu
        wd16_ref[pl.ds(pl.multiple_of(i * FFN_COL_TILE, FFN_COL_TILE), FFN_COL_TILE), :] = wd
        h = (_silu(_dot(xn_ref[...], wg)) * _dot(xn_ref[...], wu)).astype(BF16)
        acc_ref[...] += _dot(h, wd)

        @pl.when(i == n_prep - 1)
        def _():
            finish(acc_ref[...])

    @pl.when(i >= n_prep)
    def _():
        xn_ref[...] = _rms(x_ref[...], nw_ref[...]).astype(BF16)
        for c in range(n_prep):
            g = _dot(xn_ref[...], wg16_ref[c])
            u = _dot(xn_ref[...], wu16_ref[c])
            h_ref[:, c * FFN_COL_TILE:(c + 1) * FFN_COL_TILE] = (_silu(g) * u).astype(BF16)
        finish(_dot(h_ref[...], wd16_ref[...]))


def _ffn(x, norm_w, w_gate, w_up, w_down, layer, half, final_w=None):
    m, d = x.shape
    d_ff = w_gate.shape[-1]
    n_prep = d_ff // FFN_COL_TILE
    norm_row = 3 * layer + 2 * half
    row = pl.BlockSpec((FFN_ROW_TILE, d), lambda i: (jnp.maximum(i - (n_prep - 1), 0), 0))
    tile_of = lambda i: jnp.minimum(i, n_prep - 1)
    in_specs = [row,
                pl.BlockSpec((None, 1, d), lambda i: (norm_row, 0, 0)),
                pl.BlockSpec((None, None, d, FFN_COL_TILE), lambda i: (layer, half, 0, tile_of(i))),
                pl.BlockSpec((None, None, d, FFN_COL_TILE), lambda i: (layer, half, 0, tile_of(i))),
                pl.BlockSpec((None, None, FFN_COL_TILE, d), lambda i: (layer, half, tile_of(i), 0))]
    args = [x, norm_w.reshape(-1, 1, d), w_gate, w_up, w_down]
    if final_w is not None:
        in_specs.append(_resident((1, d)))
        args.append(final_w.reshape(1, d))
    return pl.pallas_call(
        functools.partial(_ffn_kernel, final_norm=final_w is not None, n_prep=n_prep),
        grid=(n_prep - 1 + m // FFN_ROW_TILE,),
        in_specs=in_specs,
        out_specs=row,
        out_shape=jax.ShapeDtypeStruct((m, d), F32),
        scratch_shapes=[pltpu.VMEM((n_prep, d, FFN_COL_TILE), BF16),
                        pltpu.VMEM((n_prep, d, FFN_COL_TILE), BF16),
                        pltpu.VMEM((d_ff, d), BF16),
                        pltpu.VMEM((FFN_ROW_TILE, d), BF16),
                        pltpu.VMEM((FFN_ROW_TILE, d_ff), BF16),
                        pltpu.VMEM((FFN_ROW_TILE, d), F32)],
        compiler_params=_params("arbitrary"),
        name="ffn",
    )(*args)


Q_A = ATTN_HEADS * HEAD_DIM
KV_A = ATTN_KV_HEADS * HEAD_DIM
QK_B = DN_HEADS * DN_DK
V_B = DN_HEADS * DN_DV
QKV_B = 2 * QK_B + V_B
MAIN_COLS = Q_A + 2 * KV_A + QKV_B + V_B


def _inproj_kernel(x_ref, nw_ref, w_ref, qa_ref, kva_ref, qkvb_ref, z_ref, gates_ref, w16_ref, wgate16_ref):
    i = pl.program_id(0)

    @pl.when(i == 0)
    def _():
        w16_ref[...] = w_ref[:, :MAIN_COLS].astype(BF16)
        wgate16_ref[...] = jnp.zeros_like(wgate16_ref)
        wgate16_ref[:, :2 * DN_HEADS] = w_ref[:, MAIN_COLS:].astype(BF16)

    @pl.when(i > 0)
    def _():
        xn = _rms(x_ref[...], nw_ref[...]).astype(BF16)
        c0, c1, c2 = Q_A, Q_A + 2 * KV_A, Q_A + 2 * KV_A + QKV_B
        qa_ref[...] = _dot(xn, w16_ref[:, :c0]).astype(BF16)
        kva_ref[...] = _dot(xn, w16_ref[:, c0:c1])
        qkvb_ref[...] = _dot(xn, w16_ref[:, c1:c2])
        z_ref[...] = _dot(xn, w16_ref[:, c2:])
        gates_ref[...] = _dot(xn, wgate16_ref[...])


def _inproj(x, norm_w, w_in, layer, e):
    m, d = x.shape
    row = lambda n: _rows_after_prep(ROW_TILE, n)
    return pl.pallas_call(
        _inproj_kernel,
        grid=(1 + m // ROW_TILE,),
        in_specs=[row(d), _layer_block((1, d), 3 * layer + 1), _layer_block(w_in.shape[1:], e)],
        out_specs=[row(Q_A), row(2 * KV_A), row(QKV_B), row(V_B), row(GATE_COLS)],
        out_shape=[jax.ShapeDtypeStruct((m, Q_A), BF16),
                   jax.ShapeDtypeStruct((m, 2 * KV_A), F32),
                   jax.ShapeDtypeStruct((m, QKV_B), F32),
                   jax.ShapeDtypeStruct((m, V_B), F32),
                   jax.ShapeDtypeStruct((m, GATE_COLS), F32)],
        scratch_shapes=[pltpu.VMEM((d, MAIN_COLS), BF16), pltpu.VMEM((d, GATE_COLS), BF16)],
        compiler_params=_params("arbitrary"),
        name="mixer_inproj",
    )(x, norm_w.reshape(-1, 1, d), w_in)


def _alibi_slopes(n_heads):
    return [float(2.0 ** (-8.0 * (h + 1) / n_heads)) for h in range(n_heads)]


ATTN_Q_TILE = 512


def _attn_kernel(sinks_ref, q_ref, kv_ref, kv_prev_ref, o_ref, *, sink_row):
    t = pl.program_id(1)
    half = lax.broadcasted_iota(jnp.int32, (1, V7X_LANES), 1) < HEAD_DIM
    kv = jnp.concatenate([kv_prev_ref[...], kv_ref[...]], axis=0)

    def placed(tile):
        swapped = pltpu.roll(tile, HEAD_DIM, axis=1)
        keep_lo = lambda x: jnp.where(half, x, 0.0).astype(BF16)
        keep_hi = lambda x: jnp.where(half, 0.0, x).astype(BF16)
        return {(0, 0): keep_lo(tile), (1, 1): keep_hi(tile), (0, 1): keep_hi(swapped), (1, 0): keep_lo(swapped)}

    k_at = placed(kv[:, :KV_A])
    v_at = placed(kv[:, KV_A:])

    i = lax.broadcasted_iota(jnp.int32, (ATTN_BLOCK, 2 * ATTN_BLOCK), 0)
    j = lax.broadcasted_iota(jnp.int32, (ATTN_BLOCK, 2 * ATTN_BLOCK), 1)
    dist = i + ATTN_BLOCK - j
    valid = jnp.where(dist >= 0, jnp.where(dist < WINDOW, 1, 0), 0) > 0
    no_prev = jnp.where(j < ATTN_BLOCK, jnp.where(t == 0, 1, 0), 0) > 0
    distf = dist.astype(F32)
    slopes = _alibi_slopes(ATTN_HEADS)
    group = ATTN_HEADS // ATTN_KV_HEADS
    bias = [jnp.where(valid, -slopes[h] * distf, -1e30) for h in range(ATTN_HEADS)]
    q = q_ref[...] * (HEAD_DIM ** -0.5)

    for qb in range(ATTN_Q_TILE // ATTN_BLOCK):
        rows = slice(qb * ATTN_BLOCK, (qb + 1) * ATTN_BLOCK)
        win = slice(qb * ATTN_BLOCK, (qb + 2) * ATTN_BLOCK)
        heads = range(ATTN_HEADS)
        place = [(h // group, h % 2) for h in heads]
        sink = [sinks_ref[sink_row, h] for h in heads]
        s = [lax.dot_general(q[rows, (h // 2) * V7X_LANES:(h // 2 + 1) * V7X_LANES], k_at[place[h]][win],
                             (((1,), (1,)), ((), ())), preferred_element_type=F32)
             + (jnp.where(no_prev, -1e30, bias[h]) if qb == 0 else bias[h]) for h in heads]
        mx = [jnp.maximum(jnp.max(s[h], axis=-1, keepdims=True), sink[h]) for h in heads]
        e = [jnp.exp(s[h] - mx[h]) for h in heads]
        inv = [1.0 / (jnp.sum(e[h], axis=-1, keepdims=True) + jnp.exp(sink[h] - mx[h])) for h in heads]
        o = [_dot(e[h].astype(BF16), v_at[place[h]][win]) * inv[h] for h in heads]
        o_ref[rows, :] = jnp.concatenate(
            [o[2 * pair] + o[2 * pair + 1] for pair in range(ATTN_HEADS // 2)], axis=1).astype(BF16)


def _attention(qa, kva, sinks, e, batch, seq):
    assert KV_A == V7X_LANES and ATTN_KV_HEADS == 2 and seq % ATTN_Q_TILE == 0
    nt = seq // ATTN_Q_TILE
    prev_blocks = ATTN_Q_TILE // ATTN_BLOCK
    row_of = lambda b, t: (b * nt + t, 0)
    return pl.pallas_call(
        functools.partial(_attn_kernel, sink_row=e),
        grid=(batch, nt),
        in_specs=[pl.BlockSpec(memory_space=pltpu.SMEM),
                  pl.BlockSpec((ATTN_Q_TILE, Q_A), row_of),
                  pl.BlockSpec((ATTN_Q_TILE, 2 * KV_A), row_of),
                  pl.BlockSpec((ATTN_BLOCK, 2 * KV_A),
                               lambda b, t: (jnp.maximum((b * nt + t) * prev_blocks - 1, 0), 0))],
        out_specs=pl.BlockSpec((ATTN_Q_TILE, Q_A), row_of),
        out_shape=jax.ShapeDtypeStruct(qa.shape, BF16),
        compiler_params=_params("parallel", "parallel"),
        name="swa_attention",
    )(sinks.astype(F32), qa, kva, kva)


PAIR = V7X_LANES // DN_DK
N_PAIRS = DN_HEADS // PAIR


def _split2(x):
    hi = x.astype(BF16)
    return hi, (x - hi.astype(F32)).astype(BF16)


def _split3(x):
    hi = x.astype(BF16)
    rest = x - hi.astype(F32)
    mid = rest.astype(BF16)
    return hi, mid, (rest - mid.astype(F32)).astype(BF16)


def _select_sum(x, onehot, splitter=_split3):
    return sum(_dot(part, onehot) for part in splitter(x))


def _group_sums(x, ones_bd):
    hi, lo = _split2(x)
    cols = []
    for j in range(x.shape[1] // V7X_LANES):
        sl = slice(j * V7X_LANES, (j + 1) * V7X_LANES)
        cols.append(_dot(hi[:, sl], ones_bd) + _dot(lo[:, sl], ones_bd))
    return jnp.concatenate(cols, axis=1)


def _bdot(a, b):
    return lax.dot_general(a, b, (((2,), (1,)), ((0,), (0,))), preferred_element_type=F32)


def _bdot_nt(a, b):
    return lax.dot_general(a, b, (((2,), (2,)), ((0,), (0,))), preferred_element_type=F32)


def _blockdiag(x, same_head):
    return jnp.where(same_head, jnp.concatenate([x, x], axis=1), 0.0).astype(BF16)


def _neumann_lower(low, same_head):
    m = -low
    m16 = m.astype(BF16)
    n = m
    m = _bdot(m16, _blockdiag(m, same_head))
    span = 2
    while span < CHUNK:
        m16 = m.astype(BF16)
        if 2 * span >= CHUNK:
            n = n + m + _bdot(n.astype(BF16), _blockdiag(m, same_head))
        else:
            p = _bdot(jnp.concatenate([n.astype(BF16), m16], axis=1), _blockdiag(m, same_head))
            n = n + m + p[:, :CHUNK]
            m = p[:, CHUNK:]
        span *= 2
    return n


def _deltanet_kernel(qkv_ref, halo_ref, z_ref, gates_ref, convw_ref, alog_ref, dt_ref, normw_ref,
                     o_ref, state_ref, xpad_ref):
    t = pl.program_id(1)
    n_seqs, tt = qkv_ref.shape[0], qkv_ref.shape[1]
    rows = n_seqs * tt
    chunks_per_seq = tt // CHUNK
    n_chunks = n_seqs * chunks_per_seq

    @pl.when(t == 0)
    def _():
        state_ref[...] = jnp.zeros_like(state_ref)
        xpad_ref[:, 0:V7X_SUBLANES, :] = jnp.zeros((n_seqs, V7X_SUBLANES, QKV_B), F32)

    @pl.when(t > 0)
    def _():
        xpad_ref[:, 0:V7X_SUBLANES, :] = halo_ref[...]

    xpad_ref[:, V7X_SUBLANES:, :] = qkv_ref[...]
    first = V7X_SUBLANES - (DN_CONV - 1)
    acc = convw_ref[DN_CONV - 1:DN_CONV, :] * xpad_ref[:, pl.ds(V7X_SUBLANES, tt), :]
    for k in range(DN_CONV - 1):
        acc = acc + convw_ref[k:k + 1, :] * xpad_ref[:, pl.ds(first + k, tt), :]
    xc = _silu(acc).reshape(rows, QKV_B)

    def iota2(shape, axis):
        return lax.broadcasted_iota(jnp.int32, shape, axis)

    sq = (V7X_LANES, V7X_LANES)
    same_head = (iota2(sq, 0) // DN_DK) == (iota2(sq, 1) // DN_DK)
    bd16 = jnp.where(same_head, 1.0, 0.0).astype(BF16)

    qk = xc[:, :2 * QK_B]
    qk = qk * lax.rsqrt(_group_sums(qk * qk, bd16) + EPS)
    q_all = qk[:, :QK_B] * (DN_DK ** -0.5)
    k_all = qk[:, QK_B:]
    v_all = xc[:, 2 * QK_B:]

    gates = gates_ref[...].reshape(rows, GATE_COLS)
    beta_g = 1.0 / (1.0 + jnp.exp(-gates))
    g_g = -jnp.exp(alog_ref[...]) * _softplus(gates + dt_ref[...])
    tril = jnp.where(iota2((CHUNK, CHUNK), 0) >= iota2((CHUNK, CHUNK), 1), 1.0, 0.0).astype(BF16)
    g_parts = _split3(g_g)
    gc_g = jnp.concatenate(
        [sum(_dot(tril, part[c * CHUNK:(c + 1) * CHUNK]) for part in g_parts) for c in range(n_chunks)], axis=0)
    ex = (GATE_COLS, V_B)
    spread_beta = jnp.where(iota2(ex, 0) == iota2(ex, 1) // DN_DK, 1.0, 0.0).astype(BF16)
    spread_g = jnp.where(iota2(ex, 0) == iota2(ex, 1) // DN_DK + DN_HEADS, 1.0, 0.0).astype(BF16)
    beta_x = _select_sum(beta_g, spread_beta)
    gc_x = _select_sum(gc_g, spread_g)

    egc_x = jnp.exp(gc_x)
    kb_all = k_all * beta_x
    vb_all = v_all * beta_x
    kbe_all = kb_all * egc_x
    qd_all = q_all * egc_x

    units = [(c, p) for c in range(n_chunks) for p in range(N_PAIRS)]
    packed = lambda x: jnp.stack(
        [x[c * CHUNK:(c + 1) * CHUNK, p * V7X_LANES:(p + 1) * V7X_LANES] for c, p in units], axis=0)
    gc = packed(gc_x)
    g_last = gc[:, CHUNK - 1:CHUNK, :]
    k = packed(k_all)
    k_dec_t = jnp.swapaxes(k * jnp.exp(g_last - gc), 1, 2).astype(BF16)
    state_gain = jnp.exp(g_last)

    ii = iota2((CHUNK, V7X_LANES), 0)
    jj = iota2((CHUNK, V7X_LANES), 1) % CHUNK
    causal = (ii >= jj)[None]
    strict = (ii > jj)[None]
    gc_row = jnp.sum(jnp.where((ii == jj)[None], gc, 0.0), axis=1, keepdims=True)
    decay = jnp.where(causal, jnp.exp(jnp.where(causal, gc - gc_row, 0.0)), 0.0)

    k_bd = _blockdiag(k, same_head)
    kq = jnp.concatenate([packed(kb_all).astype(BF16), packed(q_all).astype(BF16)], axis=1)
    kk = _bdot_nt(kq, k_bd)
    low = jnp.where(strict, kk[:, :CHUNK] * decay, 0.0)
    attn = (kk[:, CHUNK:] * decay).astype(BF16)

    n16 = _neumann_lower(low, same_head).astype(BF16)
    vb = packed(vb_all)
    kbe = packed(kbe_all)
    rhs = jnp.concatenate([_blockdiag(vb, same_head), _blockdiag(kbe, same_head)], axis=2)
    uw = _bdot(n16, rhs)
    u = vb + uw[:, :, :V7X_LANES]
    w = kbe + uw[:, :, V7X_LANES:]
    state_lhs = jnp.concatenate([w.astype(BF16), packed(qd_all).astype(BF16)], axis=1)

    low_lanes = iota2((DN_DK, V7X_LANES), 1) < DN_DV
    s = state_ref[...]
    outs = []
    for c in range(chunks_per_seq):
        at_c = lambda x: jnp.concatenate(
            [x[(sq * chunks_per_seq + c) * N_PAIRS:(sq * chunks_per_seq + c + 1) * N_PAIRS]
             for sq in range(n_seqs)], axis=0)
        ps = _bdot(at_c(state_lhs), _blockdiag(s, same_head))
        v_new = at_c(u) - ps[:, :CHUNK]
        outs.append(ps[:, CHUNK:] + _bdot(at_c(attn), _blockdiag(v_new, same_head)))
        upd = _bdot(at_c(k_dec_t), v_new.astype(BF16))
        s = s * at_c(state_gain) + jnp.where(low_lanes, upd[:, :DN_DK], upd[:, DN_DK:])
    state_ref[...] = s

    o_all = jnp.concatenate(
        [jnp.concatenate([outs[c][sq * N_PAIRS + p] for p in range(N_PAIRS)], axis=1)
         for sq in range(n_seqs) for c in range(chunks_per_seq)], axis=0)
    o_all = o_all * lax.rsqrt(_group_sums(o_all * o_all, bd16) * (1.0 / DN_DV) + EPS)
    gated = o_all * normw_ref[...] * _silu(z_ref[...].reshape(rows, V_B))
    o_ref[...] = gated.reshape(n_seqs, tt, V_B).astype(BF16)


def _deltanet(qkvb, z, gates, conv_w, a_log, dt_bias, norm_w, e, batch, seq):
    m = qkvb.shape[0]
    assert batch % DN_SEQS == 0 and seq % DN_SEQ_TILE == 0
    n_even = conv_w.shape[0]
    halo_blocks = DN_SEQ_TILE // V7X_SUBLANES
    per_seq = lambda a: a.reshape(batch, seq, a.shape[-1])
    tile = lambda n: pl.BlockSpec((DN_SEQS, DN_SEQ_TILE, n), lambda b, t: (b, t, 0))
    pad_row = lambda p: jnp.pad(p.astype(F32), ((0, 0), (DN_HEADS, GATE_COLS - 2 * DN_HEADS))).reshape(
        n_even, 1, GATE_COLS)
    out = pl.pallas_call(
        _deltanet_kernel,
        grid=(batch // DN_SEQS, seq // DN_SEQ_TILE),
        in_specs=[tile(QKV_B),
                  pl.BlockSpec((DN_SEQS, V7X_SUBLANES, QKV_B),
                               lambda b, t: (b, jnp.maximum(t * halo_blocks - 1, 0), 0)),
                  tile(V_B),
                  tile(GATE_COLS),
                  _layer_block((DN_CONV, QKV_B), e),
                  _layer_block((1, GATE_COLS), e), _layer_block((1, GATE_COLS), e),
                  _layer_block((1, V_B), e)],
        out_specs=tile(V_B),
        out_shape=jax.ShapeDtypeStruct((batch, seq, V_B), BF16),
        scratch_shapes=[pltpu.VMEM((DN_SEQS * N_PAIRS, DN_DK, V7X_LANES), F32),
                        pltpu.VMEM((DN_SEQS, DN_SEQ_TILE + V7X_SUBLANES, QKV_B), F32)],
        compiler_params=_params("parallel", "arbitrary"),
        name="gated_deltanet",
    )(per_seq(qkvb), per_seq(qkvb), per_seq(z), per_seq(gates), conv_w, pad_row(a_log), pad_row(dt_bias),
      jnp.tile(norm_w.astype(F32), (1, DN_HEADS)).reshape(n_even, 1, V_B))
    return out.reshape(m, V_B)


def _outproj_kernel(x_ref, att_ref, dn_ref, w_ref, o_ref, w16_ref):
    i = pl.program_id(0)

    @pl.when(i == 0)
    def _():
        w16_ref[...] = w_ref[...].astype(BF16)

    @pl.when(i > 0)
    def _():
        o_ref[...] = (x_ref[...] + _dot(att_ref[...], w16_ref[:Q_A, :])
                      + _dot(dn_ref[...], w16_ref[Q_A:, :]))


def _outproj(x, att, dn, w_out, e):
    m, d = x.shape
    row = lambda n: _rows_after_prep(ROW_TILE, n)
    return pl.pallas_call(
        _outproj_kernel,
        grid=(1 + m // ROW_TILE,),
        in_specs=[row(d), row(Q_A), row(V_B), _layer_block(w_out.shape[1:], e)],
        out_specs=row(d),
        out_shape=jax.ShapeDtypeStruct((m, d), F32),
        scratch_shapes=[pltpu.VMEM(w_out.shape[1:], BF16)],
        compiler_params=_params("arbitrary"),
        name="mixer_outproj",
    )(x, att, dn, w_out)


def _glu_kernel(x_ref, nw_ref, w_ref, b_ref, o_ref, w16_ref):
    i = pl.program_id(0)

    @pl.when(i == 0)
    def _():
        w16_ref[...] = w_ref[...].astype(BF16)

    @pl.when(i > 0)
    def _():
        xn = _rms(x_ref[...], nw_ref[...]).astype(BF16)
        c = o_ref.shape[1]
        for j in range(0, c, GLU_COL_TILE):
            a = _dot(xn, w16_ref[:, j:j + GLU_COL_TILE]) + b_ref[:, j:j + GLU_COL_TILE]
            g = _dot(xn, w16_ref[:, c + j:c + j + GLU_COL_TILE]) + b_ref[:, c + j:c + j + GLU_COL_TILE]
            o_ref[:, j:j + GLU_COL_TILE] = a * (1.0 / (1.0 + jnp.exp(-g)))


def _glu(x, norm_w, w, b, layer, ci):
    m, d = x.shape
    c = w.shape[2] // 2
    row = lambda n: _rows_after_prep(ROW_TILE, n)
    return pl.pallas_call(
        _glu_kernel,
        grid=(1 + m // ROW_TILE,),
        in_specs=[row(d), _layer_block((1, d), 3 * layer + 1), _layer_block(w.shape[1:], ci),
                  _layer_block((1, 2 * c), ci)],
        out_specs=row(c),
        out_shape=jax.ShapeDtypeStruct((m, c), F32),
        scratch_shapes=[pltpu.VMEM(w.shape[1:], BF16)],
        compiler_params=_params("arbitrary"),
        name="conformer_glu",
    )(x, norm_w.reshape(-1, 1, d), w, b.reshape(b.shape[0], 1, 2 * c))


CONV_HALO = 32
DWCONV_ROW_BLOCKS = 2
def _dwconv_kernel(x_ref, u_ref, halo_ref, wdw_ref, bdw_ref, lnw_ref, lnb_ref, w2_ref, b2_ref,
                   o_ref, upad_ref, shift_ref, w2_16_ref, *, width, tiles_per_seq):
    i = pl.program_id(0)

    @pl.when(i == 0)
    def _():
        w2_16_ref[...] = w2_ref[...].astype(BF16)

    @pl.when(i > 0)
    def _():
        _dwconv_tile(x_ref, u_ref, halo_ref, wdw_ref, bdw_ref, lnw_ref, lnb_ref, w2_16_ref, b2_ref,
                     o_ref, upad_ref, shift_ref, width=width, first_of_seq=(i - 1) % tiles_per_seq == 0)


def _dwconv_tile(x_ref, u_ref, halo_ref, wdw_ref, bdw_ref, lnw_ref, lnb_ref, w2_ref, b2_ref,
                 o_ref, upad_ref, shift_ref, *, width, first_of_seq):
    tt = u_ref.shape[0]

    @pl.when(first_of_seq)
    def _():
        upad_ref[0:CONV_HALO, :] = jnp.zeros((CONV_HALO, upad_ref.shape[1]), F32)

    @pl.when(jnp.logical_not(first_of_seq))
    def _():
        upad_ref[0:CONV_HALO, :] = halo_ref[...]

    upad_ref[CONV_HALO:, :] = u_ref[...]
    span = tt + CONV_HALO - V7X_SUBLANES
    for s in range(1, V7X_SUBLANES):
        shift_ref[s - 1, 0:span, :] = upad_ref[pl.ds(s, span), :]
    first = CONV_HALO - (width - 1)

    def tap_rows(k, r0, n):
        tile_row, s = divmod(first + k, V7X_SUBLANES)
        start = tile_row * V7X_SUBLANES + r0
        return upad_ref[pl.ds(start, n), :] if s == 0 else shift_ref[s - 1, pl.ds(start, n), :]

    n = tt // DWCONV_ROW_BLOCKS
    for r0 in range(0, tt, n):
        acc = bdw_ref[...]
        for k in range(width):
            acc = acc + wdw_ref[k:k + 1, :] * tap_rows(k, r0, n)
        mu = jnp.mean(acc, axis=-1, keepdims=True)
        xc = acc - mu
        y = xc * lax.rsqrt(jnp.mean(xc * xc, axis=-1, keepdims=True) + EPS) * lnw_ref[...] + lnb_ref[...]
        y = _silu(y).astype(BF16)
        o_ref[r0:r0 + n, :] = x_ref[r0:r0 + n, :] + _dot(y, w2_ref[...]) + b2_ref[...]


def _dwconv(x, u, w_dw, b_dw, ln_w, ln_b, w2, b2, ci, seq):
    m, d = x.shape
    c = u.shape[1]
    width = w_dw.shape[1]
    halo_blocks = SEQ_TILE // CONV_HALO
    vec = lambda p: p.reshape(p.shape[0], 1, p.shape[1])
    return pl.pallas_call(
        functools.partial(_dwconv_kernel, width=width, tiles_per_seq=seq // SEQ_TILE),
        grid=(1 + m // SEQ_TILE,),
        in_specs=[_rows_after_prep(SEQ_TILE, d),
                  _rows_after_prep(SEQ_TILE, c),
                  pl.BlockSpec((CONV_HALO, c), lambda i: (jnp.maximum((i - 1) * halo_blocks - 1, 0), 0)),
                  _layer_block((width, c), ci), _layer_block((1, c), ci), _layer_block((1, c), ci),
                  _layer_block((1, c), ci), _layer_block((c, d), ci), _layer_block((1, d), ci)],
        out_specs=_rows_after_prep(SEQ_TILE, d),
        out_shape=jax.ShapeDtypeStruct((m, d), F32),
        scratch_shapes=[pltpu.VMEM((SEQ_TILE + CONV_HALO, c), F32),
                        pltpu.VMEM((V7X_SUBLANES - 1, SEQ_TILE + CONV_HALO, c), F32),
                        pltpu.VMEM((c, d), BF16)],
        compiler_params=_params("arbitrary"),
        name="conformer_dwconv",
    )(x, u, u, w_dw, vec(b_dw), vec(ln_w), vec(ln_b), w2, vec(b2))


def kernel(x, norm_w, ffn_w_gate, ffn_w_up, ffn_w_down, mix_w_in, dn_conv_w, attn_sinks, dn_a_log, dn_dt_bias, dn_norm_w, mix_w_out, conv_w_pw1, conv_b_pw1, conv_w_dw, conv_b_dw, conv_ln_w, conv_ln_b, conv_w_pw2, conv_b_pw2, final_norm_w):
    batch, seq, d = x.shape
    depth = norm_w.shape[0]
    assert seq % SEQ_TILE == 0 and (batch * seq) % ROW_TILE == 0 and (batch * seq) % FFN_ROW_TILE == 0
    assert ffn_w_gate.shape[-1] % FFN_COL_TILE == 0
    assert conv_w_dw.shape[1] <= CONV_HALO + 1 and WINDOW <= ATTN_BLOCK
    assert mix_w_in.shape[2] == MAIN_COLS + 2 * DN_HEADS
    h = x.reshape(batch * seq, d)
    for layer in range(depth):
        h = _ffn(h, norm_w, ffn_w_gate, ffn_w_up, ffn_w_down, layer, 0)
        if layer % 2 == 0:
            e = layer // 2
            qa, kva, qkvb, z, gates = _inproj(h, norm_w, mix_w_in, layer, e)
            att = _attention(qa, kva, attn_sinks, e, batch, seq)
            dn = _deltanet(qkvb, z, gates, dn_conv_w, dn_a_log, dn_dt_bias, dn_norm_w, e, batch, seq)
            h = _outproj(h, att, dn, mix_w_out, e)
        else:
            c = layer // 2
            u = _glu(h, norm_w, conv_w_pw1, conv_b_pw1, layer, c)
            h = _dwconv(h, u, conv_w_dw, conv_b_dw, conv_ln_w, conv_ln_b, conv_w_pw2, conv_b_pw2, c, seq)
        last = layer == depth - 1
        h = _ffn(h, norm_w, ffn_w_gate, ffn_w_up, ffn_w_down, layer, 1,
                 final_w=final_norm_w if last else None)
    return h.reshape(batch, seq, d)
```

```python
import functools

import jax
import jax.numpy as jnp
from jax import lax
from jax.experimental import pallas as pl
from jax.experimental.pallas import tpu as pltpu

F32 = jnp.float32
BF16 = jnp.bfloat16

ATTN_HEADS = 8
ATTN_KV_HEADS = 2
HEAD_DIM = 64
WINDOW = 128
ATTN_BLOCK = 128
DN_HEADS = 8
DN_DK = 64
DN_DV = 64
DN_CONV = 4
CHUNK = 64
EPS = 1e-6

V7X_LANES = 128
V7X_SUBLANES = 8
V7X_VMEM_LIMIT_BYTES = 56 * 1024 * 1024

ROW_TILE = 1024
FFN_ROW_TILE = 1024
FFN_COL_TILE = 256
SEQ_TILE = 512
DN_SEQ_TILE = 128
DN_SEQS = 4
GATE_COLS = V7X_LANES
GLU_COL_TILE = 256


def _params(*sem):
    return pltpu.CompilerParams(dimension_semantics=sem, vmem_limit_bytes=V7X_VMEM_LIMIT_BYTES)


def _rms(x, w):
    return x * lax.rsqrt(jnp.mean(x * x, axis=-1, keepdims=True) + EPS) * w


def _silu(x):
    return x * (1.0 / (1.0 + jnp.exp(-x)))


def _softplus(x):
    return jnp.maximum(x, 0.0) + jnp.log(1.0 + jnp.exp(-jnp.abs(x)))


def _dot(a, b):
    return jnp.dot(a, b, preferred_element_type=F32)


def _resident(shape):
    return pl.BlockSpec(shape, lambda *_: (0,) * len(shape))


def _layer_block(shape, index):
    return pl.BlockSpec((None,) + tuple(shape), lambda *_: (index,) + (0,) * len(shape),
                        pipeline_mode=pl.Buffered(1))


def _rows_after_prep(n_rows, n_cols):
    return pl.BlockSpec((n_rows, n_cols), lambda i: (jnp.maximum(i - 1, 0), 0))


def _ffn_kernel(x_ref, nw_ref, wg_ref, wu_ref, wd_ref, *rest, final_norm, n_prep):
    if final_norm:
        fw_ref, o_ref, wg16_ref, wu16_ref, wd16_ref, xn_ref, h_ref, acc_ref = rest
    else:
        o_ref, wg16_ref, wu16_ref, wd16_ref, xn_ref, h_ref, acc_ref = rest
    i = pl.program_id(0)

    def finish(ffn_out):
        y = x_ref[...] + 0.5 * ffn_out
        if final_norm:
            y = _rms(y, fw_ref[...])
        o_ref[...] = y

    @pl.when(i < n_prep)
    def _():
        @pl.when(i == 0)
        def _():
            xn_ref[...] = _rms(x_ref[...], nw_ref[...]).astype(BF16)
            acc_ref[...] = jnp.zeros_like(acc_ref)

        wg, wu, wd = wg_ref[...].astype(BF16), wu_ref[...].astype(BF16), wd_ref[...].astype(BF16)
        wg16_ref[i] = wg
        wu16_ref[i] = wu
        wd16_ref[pl.ds(pl.multiple_of(i * FFN_COL_TILE, FFN_COL_TILE), FFN_COL_TILE), :] = wd
        h = (_silu(_dot(xn_ref[...], wg)) * _dot(xn_ref[...], wu)).astype(BF16)
        acc_ref[...] += _dot(h, wd)

        @pl.when(i == n_prep - 1)
        def _():
            finish(acc_ref[...])

    @pl.when(i >= n_prep)
    def _():
        xn_ref[...] = _rms(x_ref[...], nw_ref[...]).astype(BF16)
        for c in range(n_prep):
            g = _dot(xn_ref[...], wg16_ref[c])
            u = _dot(xn_ref[...], wu16_ref[c])
            h_ref[:, c * FFN_COL_TILE:(c + 1) * FFN_COL_TILE] = (_silu(g) * u).astype(BF16)
        finish(_dot(h_ref[...], wd16_ref[...]))


def _ffn(x, norm_w, w_gate, w_up, w_down, layer, half, final_w=None):
    m, d = x.shape
    d_ff = w_gate.shape[-1]
    n_prep = d_ff // FFN_COL_TILE
    norm_row = 3 * layer + 2 * half
    row = pl.BlockSpec((FFN_ROW_TILE, d), lambda i: (jnp.maximum(i - (n_prep - 1), 0), 0))
    tile_of = lambda i: jnp.minimum(i, n_prep - 1)
    in_specs = [row,
                pl.BlockSpec((None, 1, d), lambda i: (norm_row, 0, 0)),
                pl.BlockSpec((None, None, d, FFN_COL_TILE), lambda i: (layer, half, 0, tile_of(i))),
                pl.BlockSpec((None, None, d, FFN_COL_TILE), lambda i: (layer, half, 0, tile_of(i))),
                pl.BlockSpec((None, None, FFN_COL_TILE, d), lambda i: (layer, half, tile_of(i), 0))]
    args = [x, norm_w.reshape(-1, 1, d), w_gate, w_up, w_down]
    if final_w is not None:
        in_specs.append(_resident((1, d)))
        args.append(final_w.reshape(1, d))
    return pl.pallas_call(
        functools.partial(_ffn_kernel, final_norm=final_w is not None, n_prep=n_prep),
        grid=(n_prep - 1 + m // FFN_ROW_TILE,),
        in_specs=in_specs,
        out_specs=row,
        out_shape=jax.ShapeDtypeStruct((m, d), F32),
        scratch_shapes=[pltpu.VMEM((n_prep, d, FFN_COL_TILE), BF16),
                        pltpu.VMEM((n_prep, d, FFN_COL_TILE), BF16),
                        pltpu.VMEM((d_ff, d), BF16),
                        pltpu.VMEM((FFN_ROW_TILE, d), BF16),
                        pltpu.VMEM((FFN_ROW_TILE, d_ff), BF16),
                        pltpu.VMEM((FFN_ROW_TILE, d), F32)],
        compiler_params=_params("arbitrary"),
        name="ffn",
    )(*args)


Q_A = ATTN_HEADS * HEAD_DIM
KV_A = ATTN_KV_HEADS * HEAD_DIM
QK_B = DN_HEADS * DN_DK
V_B = DN_HEADS * DN_DV
QKV_B = 2 * QK_B + V_B
MAIN_COLS = Q_A + 2 * KV_A + QKV_B + V_B


def _inproj_kernel(x_ref, nw_ref, w_ref, qa_ref, kva_ref, qkvb_ref, z_ref, gates_ref, w16_ref, wgate16_ref):
    i = pl.program_id(0)

    @pl.when(i == 0)
    def _():
        w16_ref[...] = w_ref[:, :MAIN_COLS].astype(BF16)
        wgate16_ref[...] = jnp.zeros_like(wgate16_ref)
        wgate16_ref[:, :2 * DN_HEADS] = w_ref[:, MAIN_COLS:].astype(BF16)

    @pl.when(i > 0)
    def _():
        xn = _rms(x_ref[...], nw_ref[...]).astype(BF16)
        c0, c1, c2 = Q_A, Q_A + 2 * KV_A, Q_A + 2 * KV_A + QKV_B
        qa_ref[...] = _dot(xn, w16_ref[:, :c0]).astype(BF16)
        kva_ref[...] = _dot(xn, w16_ref[:, c0:c1])
        qkvb_ref[...] = _dot(xn, w16_ref[:, c1:c2])
        z_ref[...] = _dot(xn, w16_ref[:, c2:])
        gates_ref[...] = _dot(xn, wgate16_ref[...])


def _inproj(x, norm_w, w_in, layer, e):
    m, d = x.shape
    row = lambda n: _rows_after_prep(ROW_TILE, n)
    return pl.pallas_call(
        _inproj_kernel,
        grid=(1 + m // ROW_TILE,),
        in_specs=[row(d), _layer_block((1, d), 3 * layer + 1), _layer_block(w_in.shape[1:], e)],
        out_specs=[row(Q_A), row(2 * KV_A), row(QKV_B), row(V_B), row(GATE_COLS)],
        out_shape=[jax.ShapeDtypeStruct((m, Q_A), BF16),
                   jax.ShapeDtypeStruct((m, 2 * KV_A), F32),
                   jax.ShapeDtypeStruct((m, QKV_B), F32),
                   jax.ShapeDtypeStruct((m, V_B), F32),
                   jax.ShapeDtypeStruct((m, GATE_COLS), F32)],
        scratch_shapes=[pltpu.VMEM((d, MAIN_COLS), BF16), pltpu.VMEM((d, GATE_COLS), BF16)],
        compiler_params=_params("arbitrary"),
        name="mixer_inproj",
    )(x, norm_w.reshape(-1, 1, d), w_in)


def _alibi_slopes(n_heads):
    return [float(2.0 ** (-8.0 * (h + 1) / n_heads)) for h in range(n_heads)]


ATTN_Q_TILE = 512


def _attn_kernel(sinks_ref, q_ref, kv_ref, kv_prev_ref, o_ref, *, sink_row):
    t = pl.program_id(1)
    half = lax.broadcasted_iota(jnp.int32, (1, V7X_LANES), 1) < HEAD_DIM
    kv = jnp.concatenate([kv_prev_ref[...], kv_ref[...]], axis=0)

    def placed(tile):
        swapped = pltpu.roll(tile, HEAD_DIM, axis=1)
        keep_lo = lambda x: jnp.where(half, x, 0.0).astype(BF16)
        keep_hi = lambda x: jnp.where(half, 0.0, x).astype(BF16)
        return {(0, 0): keep_lo(tile), (1, 1): keep_hi(tile), (0, 1): keep_hi(swapped), (1, 0): keep_lo(swapped)}

    k_at = placed(kv[:, :KV_A])
    v_at = placed(kv[:, KV_A:])

    i = lax.broadcasted_iota(jnp.int32, (ATTN_BLOCK, 2 * ATTN_BLOCK), 0)
    j = lax.broadcasted_iota(jnp.int32, (ATTN_BLOCK, 2 * ATTN_BLOCK), 1)
    dist = i + ATTN_BLOCK - j
    valid = jnp.where(dist >= 0, jnp.where(dist < WINDOW, 1, 0), 0) > 0
    no_prev = jnp.where(j < ATTN_BLOCK, jnp.where(t == 0, 1, 0), 0) > 0
    distf = dist.astype(F32)
    slopes = _alibi_slopes(ATTN_HEADS)
    group = ATTN_HEADS // ATTN_KV_HEADS
    bias = [jnp.where(valid, -slopes[h] * distf, -1e30) for h in range(ATTN_HEADS)]
    q = q_ref[...] * (HEAD_DIM ** -0.5)

    for qb in range(ATTN_Q_TILE // ATTN_BLOCK):
        rows = slice(qb * ATTN_BLOCK, (qb + 1) * ATTN_BLOCK)
        win = slice(qb * ATTN_BLOCK, (qb + 2) * ATTN_BLOCK)
        heads = range(ATTN_HEADS)
        place = [(h // group, h % 2) for h in heads]
        sink = [sinks_ref[sink_row, h] for h in heads]
        s = [lax.dot_general(q[rows, (h // 2) * V7X_LANES:(h // 2 + 1) * V7X_LANES], k_at[place[h]][win],
                             (((1,), (1,)), ((), ())), preferred_element_type=F32)
             + (jnp.where(no_prev, -1e30, bias[h]) if qb == 0 else bias[h]) for h in heads]
        mx = [jnp.maximum(jnp.max(s[h], axis=-1, keepdims=True), sink[h]) for h in heads]
        e = [jnp.exp(s[h] - mx[h]) for h in heads]
        inv = [1.0 / (jnp.sum(e[h], axis=-1, keepdims=True) + jnp.exp(sink[h] - mx[h])) for h in heads]
        o = [_dot(e[h].astype(BF16), v_at[place[h]][win]) * inv[h] for h in heads]
        o_ref[rows, :] = jnp.concatenate(
            [o[2 * pair] + o[2 * pair + 1] for pair in range(ATTN_HEADS // 2)], axis=1).astype(BF16)


def _attention(qa, kva, sinks, e, batch, seq):
    assert KV_A == V7X_LANES and ATTN_KV_HEADS == 2 and seq % ATTN_Q_TILE == 0
    nt = seq // ATTN_Q_TILE
    prev_blocks = ATTN_Q_TILE // ATTN_BLOCK
    row_of = lambda b, t: (b * nt + t, 0)
    return pl.pallas_call(
        functools.partial(_attn_kernel, sink_row=e),
        grid=(batch, nt),
        in_specs=[pl.BlockSpec(memory_space=pltpu.SMEM),
                  pl.BlockSpec((ATTN_Q_TILE, Q_A), row_of),
                  pl.BlockSpec((ATTN_Q_TILE, 2 * KV_A), row_of),
                  pl.BlockSpec((ATTN_BLOCK, 2 * KV_A),
                               lambda b, t: (jnp.maximum((b * nt + t) * prev_blocks - 1, 0), 0))],
        out_specs=pl.BlockSpec((ATTN_Q_TILE, Q_A), row_of),
        out_shape=jax.ShapeDtypeStruct(qa.shape, BF16),
        compiler_params=_params("parallel", "parallel"),
        name="swa_attention",
    )(sinks.astype(F32), qa, kva, kva)


PAIR = V7X_LANES // DN_DK
N_PAIRS = DN_HEADS // PAIR


def _split2(x):
    hi = x.astype(BF16)
    return hi, (x - hi.astype(F32)).astype(BF16)


def _split3(x):
    hi = x.astype(BF16)
    rest = x - hi.astype(F32)
    mid = rest.astype(BF16)
    return hi, mid, (rest - mid.astype(F32)).astype(BF16)


def _select_sum(x, onehot, splitter=_split3):
    return sum(_dot(part, onehot) for part in splitter(x))


def _group_sums(x, ones_bd):
    hi, lo = _split2(x)
    cols = []
    for j in range(x.shape[1] // V7X_LANES):
        sl = slice(j * V7X_LANES, (j + 1) * V7X_LANES)
        cols.append(_dot(hi[:, sl], ones_bd) + _dot(lo[:, sl], ones_bd))
    return jnp.concatenate(cols, axis=1)


def _bdot(a, b):
    return lax.dot_general(a, b, (((2,), (1,)), ((0,), (0,))), preferred_element_type=F32)


def _bdot_nt(a, b):
    return lax.dot_general(a, b, (((2,), (2,)), ((0,), (0,))), preferred_element_type=F32)


def _blockdiag(x, same_head):
    return jnp.where(same_head, jnp.concatenate([x, x], axis=1), 0.0).astype(BF16)


def _neumann_lower(low, same_head):
    m = -low
    m16 = m.astype(BF16)
    n = m
    m = _bdot(m16, _blockdiag(m, same_head))
    span = 2
    while span < CHUNK:
        m16 = m.astype(BF16)
        if 2 * span >= CHUNK:
            n = n + m + _bdot(n.astype(BF16), _blockdiag(m, same_head))
        else:
            p = _bdot(jnp.concatenate([n.astype(BF16), m16], axis=1), _blockdiag(m, same_head))
            n = n + m + p[:, :CHUNK]
            m = p[:, CHUNK:]
        span *= 2
    return n


def _deltanet_kernel(qkv_ref, halo_ref, z_ref, gates_ref, convw_ref, alog_ref, dt_ref, normw_ref,
                     o_ref, state_ref, xpad_ref):
    t = pl.program_id(1)
    n_seqs, tt = qkv_ref.shape[0], qkv_ref.shape[1]
    rows = n_seqs * tt
    chunks_per_seq = tt // CHUNK
    n_chunks = n_seqs * chunks_per_seq

    @pl.when(t == 0)
    def _():
        state_ref[...] = jnp.zeros_like(state_ref)
        xpad_ref[:, 0:V7X_SUBLANES, :] = jnp.zeros((n_seqs, V7X_SUBLANES, QKV_B), F32)

    @pl.when(t > 0)
    def _():
        xpad_ref[:, 0:V7X_SUBLANES, :] = halo_ref[...]

    xpad_ref[:, V7X_SUBLANES:, :] = qkv_ref[...]
    first = V7X_SUBLANES - (DN_CONV - 1)
    acc = convw_ref[DN_CONV - 1:DN_CONV, :] * xpad_ref[:, pl.ds(V7X_SUBLANES, tt), :]
    for k in range(DN_CONV - 1):
        acc = acc + convw_ref[k:k + 1, :] * xpad_ref[:, pl.ds(first + k, tt), :]
    xc = _silu(acc).reshape(rows, QKV_B)

    def iota2(shape, axis):
        return lax.broadcasted_iota(jnp.int32, shape, axis)

    sq = (V7X_LANES, V7X_LANES)
    same_head = (iota2(sq, 0) // DN_DK) == (iota2(sq, 1) // DN_DK)
    bd16 = jnp.where(same_head, 1.0, 0.0).astype(BF16)

    qk = xc[:, :2 * QK_B]
    qk = qk * lax.rsqrt(_group_sums(qk * qk, bd16) + EPS)
    q_all = qk[:, :QK_B] * (DN_DK ** -0.5)
    k_all = qk[:, QK_B:]
    v_all = xc[:, 2 * QK_B:]

    gates = gates_ref[...].reshape(rows, GATE_COLS)
    beta_g = 1.0 / (1.0 + jnp.exp(-gates))
    g_g = -jnp.exp(alog_ref[...]) * _softplus(gates + dt_ref[...])
    tril = jnp.where(iota2((CHUNK, CHUNK), 0) >= iota2((CHUNK, CHUNK), 1), 1.0, 0.0).astype(BF16)
    g_parts = _split3(g_g)
    gc_g = jnp.concatenate(
        [sum(_dot(tril, part[c * CHUNK:(c + 1) * CHUNK]) for part in g_parts) for c in range(n_chunks)], axis=0)
    ex = (GATE_COLS, V_B)
    spread_beta = jnp.where(iota2(ex, 0) == iota2(ex, 1) // DN_DK, 1.0, 0.0).astype(BF16)
    spread_g = jnp.where(iota2(ex, 0) == iota2(ex, 1) // DN_DK + DN_HEADS, 1.0, 0.0).astype(BF16)
    beta_x = _select_sum(beta_g, spread_beta)
    gc_x = _select_sum(gc_g, spread_g)

    egc_x = jnp.exp(gc_x)
    kb_all = k_all * beta_x
    vb_all = v_all * beta_x
    kbe_all = kb_all * egc_x
    qd_all = q_all * egc_x

    units = [(c, p) for c in range(n_chunks) for p in range(N_PAIRS)]
    packed = lambda x: jnp.stack(
        [x[c * CHUNK:(c + 1) * CHUNK, p * V7X_LANES:(p + 1) * V7X_LANES] for c, p in units], axis=0)
    gc = packed(gc_x)
    g_last = gc[:, CHUNK - 1:CHUNK, :]
    k = packed(k_all)
    k_dec_t = jnp.swapaxes(k * jnp.exp(g_last - gc), 1, 2).astype(BF16)
    state_gain = jnp.exp(g_last)

    ii = iota2((CHUNK, V7X_LANES), 0)
    jj = iota2((CHUNK, V7X_LANES), 1) % CHUNK
    causal = (ii >= jj)[None]
    strict = (ii > jj)[None]
    gc_row = jnp.sum(jnp.where((ii == jj)[None], gc, 0.0), axis=1, keepdims=True)
    decay = jnp.where(causal, jnp.exp(jnp.where(causal, gc - gc_row, 0.0)), 0.0)

    k_bd = _blockdiag(k, same_head)
    kq = jnp.concatenate([packed(kb_all).astype(BF16), packed(q_all).astype(BF16)], axis=1)
    kk = _bdot_nt(kq, k_bd)
    low = jnp.where(strict, kk[:, :CHUNK] * decay, 0.0)
    attn = (kk[:, CHUNK:] * decay).astype(BF16)

    n16 = _neumann_lower(low, same_head).astype(BF16)
    vb = packed(vb_all)
    kbe = packed(kbe_all)
    rhs = jnp.concatenate([_blockdiag(vb, same_head), _blockdiag(kbe, same_head)], axis=2)
    uw = _bdot(n16, rhs)
    u = vb + uw[:, :, :V7X_LANES]
    w = kbe + uw[:, :, V7X_LANES:]
    state_lhs = jnp.concatenate([w.astype(BF16), packed(qd_all).astype(BF16)], axis=1)

    low_lanes = iota2((DN_DK, V7X_LANES), 1) < DN_DV
    s = state_ref[...]
    outs = []
    for c in range(chunks_per_seq):
        at_c = lambda x: jnp.concatenate(
            [x[(sq * chunks_per_seq + c) * N_PAIRS:(sq * chunks_per_seq + c + 1) * N_PAIRS]
             for sq in range(n_seqs)], axis=0)
        ps = _bdot(at_c(state_lhs), _blockdiag(s, same_head))
        v_new = at_c(u) - ps[:, :CHUNK]
        outs.append(ps[:, CHUNK:] + _bdot(at_c(attn), _blockdiag(v_new, same_head)))
        upd = _bdot(at_c(k_dec_t), v_new.astype(BF16))
        s = s * at_c(state_gain) + jnp.where(low_lanes, upd[:, :DN_DK], upd[:, DN_DK:])
    state_ref[...] = s

    o_all = jnp.concatenate(
        [jnp.concatenate([outs[c][sq * N_PAIRS + p] for p in range(N_PAIRS)], axis=1)
         for sq in range(n_seqs) for c in range(chunks_per_seq)], axis=0)
    o_all = o_all * lax.rsqrt(_group_sums(o_all * o_all, bd16) * (1.0 / DN_DV) + EPS)
    gated = o_all * normw_ref[...] * _silu(z_ref[...].reshape(rows, V_B))
    o_ref[...] = gated.reshape(n_seqs, tt, V_B).astype(BF16)


def _deltanet(qkvb, z, gates, conv_w, a_log, dt_bias, norm_w, e, batch, seq):
    m = qkvb.shape[0]
    assert batch % DN_SEQS == 0 and seq % DN_SEQ_TILE == 0
    n_even = conv_w.shape[0]
    halo_blocks = DN_SEQ_TILE // V7X_SUBLANES
    per_seq = lambda a: a.reshape(batch, seq, a.shape[-1])
    tile = lambda n: pl.BlockSpec((DN_SEQS, DN_SEQ_TILE, n), lambda b, t: (b, t, 0))
    pad_row = lambda p: jnp.pad(p.astype(F32), ((0, 0), (DN_HEADS, GATE_COLS - 2 * DN_HEADS))).reshape(
        n_even, 1, GATE_COLS)
    out = pl.pallas_call(
        _deltanet_kernel,
        grid=(batch // DN_SEQS, seq // DN_SEQ_TILE),
        in_specs=[tile(QKV_B),
                  pl.BlockSpec((DN_SEQS, V7X_SUBLANES, QKV_B),
                               lambda b, t: (b, jnp.maximum(t * halo_blocks - 1, 0), 0)),
                  tile(V_B),
                  tile(GATE_COLS),
                  _layer_block((DN_CONV, QKV_B), e),
                  _layer_block((1, GATE_COLS), e), _layer_block((1, GATE_COLS), e),
                  _layer_block((1, V_B), e)],
        out_specs=tile(V_B),
        out_shape=jax.ShapeDtypeStruct((batch, seq, V_B), BF16),
        scratch_shapes=[pltpu.VMEM((DN_SEQS * N_PAIRS, DN_DK, V7X_LANES), F32),
                        pltpu.VMEM((DN_SEQS, DN_SEQ_TILE + V7X_SUBLANES, QKV_B), F32)],
        compiler_params=_params("parallel", "arbitrary"),
        name="gated_deltanet",
    )(per_seq(qkvb), per_seq(qkvb), per_seq(z), per_seq(gates), conv_w, pad_row(a_log), pad_row(dt_bias),
      jnp.tile(norm_w.astype(F32), (1, DN_HEADS)).reshape(n_even, 1, V_B))
    return out.reshape(m, V_B)


def _outproj_kernel(x_ref, att_ref, dn_ref, w_ref, o_ref, w16_ref):
    i = pl.program_id(0)

    @pl.when(i == 0)
    def _():
        w16_ref[...] = w_ref[...].astype(BF16)

    @pl.when(i > 0)
    def _():
        o_ref[...] = (x_ref[...] + _dot(att_ref[...], w16_ref[:Q_A, :])
                      + _dot(dn_ref[...], w16_ref[Q_A:, :]))


def _outproj(x, att, dn, w_out, e):
    m, d = x.shape
    row = lambda n: _rows_after_prep(ROW_TILE, n)
    return pl.pallas_call(
        _outproj_kernel,
        grid=(1 + m // ROW_TILE,),
        in_specs=[row(d), row(Q_A), row(V_B), _layer_block(w_out.shape[1:], e)],
        out_specs=row(d),
        out_shape=jax.ShapeDtypeStruct((m, d), F32),
        scratch_shapes=[pltpu.VMEM(w_out.shape[1:], BF16)],
        compiler_params=_params("arbitrary"),
        name="mixer_outproj",
    )(x, att, dn, w_out)


def _glu_kernel(x_ref, nw_ref, w_ref, b_ref, o_ref, w16_ref):
    i = pl.program_id(0)

    @pl.when(i == 0)
    def _():
        w16_ref[...] = w_ref[...].astype(BF16)

    @pl.when(i > 0)
    def _():
        xn = _rms(x_ref[...], nw_ref[...]).astype(BF16)
        c = o_ref.shape[1]
        for j in range(0, c, GLU_COL_TILE):
            a = _dot(xn, w16_ref[:, j:j + GLU_COL_TILE]) + b_ref[:, j:j + GLU_COL_TILE]
            g = _dot(xn, w16_ref[:, c + j:c + j + GLU_COL_TILE]) + b_ref[:, c + j:c + j + GLU_COL_TILE]
            o_ref[:, j:j + GLU_COL_TILE] = a * (1.0 / (1.0 + jnp.exp(-g)))


def _glu(x, norm_w, w, b, layer, ci):
    m, d = x.shape
    c = w.shape[2] // 2
    row = lambda n: _rows_after_prep(ROW_TILE, n)
    return pl.pallas_call(
        _glu_kernel,
        grid=(1 + m // ROW_TILE,),
        in_specs=[row(d), _layer_block((1, d), 3 * layer + 1), _layer_block(w.shape[1:], ci),
                  _layer_block((1, 2 * c), ci)],
        out_specs=row(c),
        out_shape=jax.ShapeDtypeStruct((m, c), F32),
        scratch_shapes=[pltpu.VMEM(w.shape[1:], BF16)],
        compiler_params=_params("arbitrary"),
        name="conformer_glu",
    )(x, norm_w.reshape(-1, 1, d), w, b.reshape(b.shape[0], 1, 2 * c))


CONV_HALO = 32


def _dwconv_kernel(x_ref, u_ref, halo_ref, wdw_ref, bdw_ref, lnw_ref, lnb_ref, w2_ref, b2_ref,
                   o_ref, upad_ref, shift_ref, w2_16_ref, *, width, tiles_per_seq):
    i = pl.program_id(0)

    @pl.when(i == 0)
    def _():
        w2_16_ref[...] = w2_ref[...].astype(BF16)

    @pl.when(i > 0)
    def _():
        _dwconv_tile(x_ref, u_ref, halo_ref, wdw_ref, bdw_ref, lnw_ref, lnb_ref, w2_16_ref, b2_ref,
                     o_ref, upad_ref, shift_ref, width=width, first_of_seq=(i - 1) % tiles_per_seq == 0)


def _dwconv_tile(x_ref, u_ref, halo_ref, wdw_ref, bdw_ref, lnw_ref, lnb_ref, w2_ref, b2_ref,
                 o_ref, upad_ref, shift_ref, *, width, first_of_seq):
    tt = u_ref.shape[0]

    @pl.when(first_of_seq)
    def _():
        upad_ref[0:CONV_HALO, :] = jnp.zeros((CONV_HALO, upad_ref.shape[1]), F32)

    @pl.when(jnp.logical_not(first_of_seq))
    def _():
        upad_ref[0:CONV_HALO, :] = halo_ref[...]

    upad_ref[CONV_HALO:, :] = u_ref[...]
    span = tt + CONV_HALO - V7X_SUBLANES
    for s in range(1, V7X_SUBLANES):
        shift_ref[s - 1, 0:span, :] = upad_ref[pl.ds(s, span), :]
    first = CONV_HALO - (width - 1)

    def tap_rows(k):
        tile_row, s = divmod(first + k, V7X_SUBLANES)
        start = tile_row * V7X_SUBLANES
        return upad_ref[pl.ds(start, tt), :] if s == 0 else shift_ref[s - 1, pl.ds(start, tt), :]

    acc = bdw_ref[...]
    for k in range(width):
        acc = acc + wdw_ref[k:k + 1, :] * tap_rows(k)
    mu = jnp.mean(acc, axis=-1, keepdims=True)
    xc = acc - mu
    y = xc * lax.rsqrt(jnp.mean(xc * xc, axis=-1, keepdims=True) + EPS) * lnw_ref[...] + lnb_ref[...]
    y = _silu(y).astype(BF16)
    o_ref[...] = x_ref[...] + _dot(y, w2_ref[...]) + b2_ref[...]


def _dwconv(x, u, w_dw, b_dw, ln_w, ln_b, w2, b2, ci, seq):
    m, d = x.shape
    c = u.shape[1]
    width = w_dw.shape[1]
    halo_blocks = SEQ_TILE // CONV_HALO
    vec = lambda p: p.reshape(p.shape[0], 1, p.shape[1])
    return pl.pallas_call(
        functools.partial(_dwconv_kernel, width=width, tiles_per_seq=seq // SEQ_TILE),
        grid=(1 + m // SEQ_TILE,),
        in_specs=[_rows_after_prep(SEQ_TILE, d),
                  _rows_after_prep(SEQ_TILE, c),
                  pl.BlockSpec((CONV_HALO, c), lambda i: (jnp.maximum((i - 1) * halo_blocks - 1, 0), 0)),
                  _layer_block((width, c), ci), _layer_block((1, c), ci), _layer_block((1, c), ci),
                  _layer_block((1, c), ci), _layer_block((c, d), ci), _layer_block((1, d), ci)],
        out_specs=_rows_after_prep(SEQ_TILE, d),
        out_shape=jax.ShapeDtypeStruct((m, d), F32),
        scratch_shapes=[pltpu.VMEM((SEQ_TILE + CONV_HALO, c), F32),
                        pltpu.VMEM((V7X_SUBLANES - 1, SEQ_TILE + CONV_HALO, c), F32),
                        pltpu.VMEM((c, d), BF16)],
        compiler_params=_params("arbitrary"),
        name="conformer_dwconv",
    )(x, u, u, w_dw, vec(b_dw), vec(ln_w), vec(ln_b), w2, vec(b2))


def kernel(x, norm_w, ffn_w_gate, ffn_w_up, ffn_w_down, mix_w_in, dn_conv_w, attn_sinks, dn_a_log, dn_dt_bias, dn_norm_w, mix_w_out, conv_w_pw1, conv_b_pw1, conv_w_dw, conv_b_dw, conv_ln_w, conv_ln_b, conv_w_pw2, conv_b_pw2, final_norm_w):
    batch, seq, d = x.shape
    depth = norm_w.shape[0]
    assert seq % SEQ_TILE == 0 and (batch * seq) % ROW_TILE == 0 and (batch * seq) % FFN_ROW_TILE == 0
    assert ffn_w_gate.shape[-1] % FFN_COL_TILE == 0
    assert conv_w_dw.shape[1] <= CONV_HALO + 1 and WINDOW <= ATTN_BLOCK
    assert mix_w_in.shape[2] == MAIN_COLS + 2 * DN_HEADS
    h = x.reshape(batch * seq, d)
    for layer in range(depth):
        h = _ffn(h, norm_w, ffn_w_gate, ffn_w_up, ffn_w_down, layer, 0)
        if layer % 2 == 0:
            e = layer // 2
            qa, kva, qkvb, z, gates = _inproj(h, norm_w, mix_w_in, layer, e)
            att = _attention(qa, kva, attn_sinks, e, batch, seq)
            dn = _deltanet(qkvb, z, gates, dn_conv_w, dn_a_log, dn_dt_bias, dn_norm_w, e, batch, seq)
            h = _outproj(h, att, dn, mix_w_out, e)
        else:
            c = layer // 2
            u = _glu(h, norm_w, conv_w_pw1, conv_b_pw1, layer, c)
            h = _dwconv(h, u, conv_w_dw, conv_b_dw, conv_ln_w, conv_ln_b, conv_w_pw2, conv_b_pw2, c, seq)
        last = layer == depth - 1
        h = _ffn(h, norm_w, ffn_w_gate, ffn_w_up, ffn_w_down, layer, 1,
                 final_w=final_norm_w if last else None)
    return h.reshape(batch, seq, d)
```

```python
import functools

import jax
import jax.numpy as jnp
from jax import lax
from jax.experimental import pallas as pl
from jax.experimental.pallas import tpu as pltpu

F32 = jnp.float32
BF16 = jnp.bfloat16

ATTN_HEADS = 8
ATTN_KV_HEADS = 2
HEAD_DIM = 64
WINDOW = 128
ATTN_BLOCK = 128
DN_HEADS = 8
DN_DK = 64
DN_DV = 64
DN_CONV = 4
CHUNK = 64
EPS = 1e-6

V7X_LANES = 128
V7X_SUBLANES = 8
V7X_VMEM_LIMIT_BYTES = 56 * 1024 * 1024

ROW_TILE = 1024
FFN_ROW_TILE = 1024
FFN_COL_TILE = 256
SEQ_TILE = 512
DN_SEQ_TILE = 128
DN_SEQS = 4
GATE_COLS = V7X_LANES
GLU_COL_TILE = 256
INPROJ_STAGE_COLS = 256


def _params(*sem):
    return pltpu.CompilerParams(dimension_semantics=sem, vmem_limit_bytes=V7X_VMEM_LIMIT_BYTES)


def _rms(x, w):
    return x * lax.rsqrt(jnp.mean(x * x, axis=-1, keepdims=True) + EPS) * w


def _silu(x):
    return x * (1.0 / (1.0 + jnp.exp(-x)))


def _softplus(x):
    return jnp.maximum(x, 0.0) + jnp.log(1.0 + jnp.exp(-jnp.abs(x)))


def _dot(a, b):
    return jnp.dot(a, b, preferred_element_type=F32)


def _resident(shape):
    return pl.BlockSpec(shape, lambda *_: (0,) * len(shape))


def _layer_block(shape, index):
    return pl.BlockSpec((None,) + tuple(shape), lambda *_: (index,) + (0,) * len(shape),
                        pipeline_mode=pl.Buffered(1))


def _rows_after_prep(n_rows, n_cols):
    return pl.BlockSpec((n_rows, n_cols), lambda i: (jnp.maximum(i - 1, 0), 0))


def _ffn_kernel(x_ref, nw_ref, wg_ref, wu_ref, wd_ref, *rest, final_norm, n_prep):
    if final_norm:
        fw_ref, o_ref, wg16_ref, wu16_ref, wd16_ref, xn_ref, h_ref, acc_ref = rest
    else:
        o_ref, wg16_ref, wu16_ref, wd16_ref, xn_ref, h_ref, acc_ref = rest
    i = pl.program_id(0)

    def finish(ffn_out):
        y = x_ref[...] + 0.5 * ffn_out
        if final_norm:
            y = _rms(y, fw_ref[...])
        o_ref[...] = y

    @pl.when(i < n_prep)
    def _():
        @pl.when(i == 0)
        def _():
            xn_ref[...] = _rms(x_ref[...], nw_ref[...]).astype(BF16)
            acc_ref[...] = jnp.zeros_like(acc_ref)

        wg, wu, wd = wg_ref[...].astype(BF16), wu_ref[...].astype(BF16), wd_ref[...].astype(BF16)
        wg16_ref[i] = wg
        wu16_ref[i] = wu
        wd16_ref[pl.ds(pl.multiple_of(i * FFN_COL_TILE, FFN_COL_TILE), FFN_COL_TILE), :] = wd
        h = (_silu(_dot(xn_ref[...], wg)) * _dot(xn_ref[...], wu)).astype(BF16)
        acc_ref[...] += _dot(h, wd)

        @pl.when(i == n_prep - 1)
        def _():
            finish(acc_ref[...])

    @pl.when(i >= n_prep)
    def _():
        xn_ref[...] = _rms(x_ref[...], nw_ref[...]).astype(BF16)
        for c in range(n_prep):
            g = _dot(xn_ref[...], wg16_ref[c])
            u = _dot(xn_ref[...], wu16_ref[c])
            h_ref[:, c * FFN_COL_TILE:(c + 1) * FFN_COL_TILE] = (_silu(g) * u).astype(BF16)
        finish(_dot(h_ref[...], wd16_ref[...]))


def _ffn(x, norm_w, w_gate, w_up, w_down, layer, half, final_w=None):
    m, d = x.shape
    d_ff = w_gate.shape[-1]
    n_prep = d_ff // FFN_COL_TILE
    norm_row = 3 * layer + 2 * half
    row = pl.BlockSpec((FFN_ROW_TILE, d), lambda i: (jnp.maximum(i - (n_prep - 1), 0), 0))
    tile_of = lambda i: jnp.minimum(i, n_prep - 1)
    in_specs = [row,
                pl.BlockSpec((None, 1, d), lambda i: (norm_row, 0, 0)),
                pl.BlockSpec((None, None, d, FFN_COL_TILE), lambda i: (layer, half, 0, tile_of(i))),
                pl.BlockSpec((None, None, d, FFN_COL_TILE), lambda i: (layer, half, 0, tile_of(i))),
                pl.BlockSpec((None, None, FFN_COL_TILE, d), lambda i: (layer, half, tile_of(i), 0))]
    args = [x, norm_w.reshape(-1, 1, d), w_gate, w_up, w_down]
    if final_w is not None:
        in_specs.append(_resident((1, d)))
        args.append(final_w.reshape(1, d))
    return pl.pallas_call(
        functools.partial(_ffn_kernel, final_norm=final_w is not None, n_prep=n_prep),
        grid=(n_prep - 1 + m // FFN_ROW_TILE,),
        in_specs=in_specs,
        out_specs=row,
        out_shape=jax.ShapeDtypeStruct((m, d), F32),
        scratch_shapes=[pltpu.VMEM((n_prep, d, FFN_COL_TILE), BF16),
                        pltpu.VMEM((n_prep, d, FFN_COL_TILE), BF16),
                        pltpu.VMEM((d_ff, d), BF16),
                        pltpu.VMEM((FFN_ROW_TILE, d), BF16),
                        pltpu.VMEM((FFN_ROW_TILE, d_ff), BF16),
                        pltpu.VMEM((FFN_ROW_TILE, d), F32)],
        compiler_params=_params("arbitrary"),
        name="ffn",
    )(*args)


Q_A = ATTN_HEADS * HEAD_DIM
KV_A = ATTN_KV_HEADS * HEAD_DIM
QK_B = DN_HEADS * DN_DK
V_B = DN_HEADS * DN_DV
QKV_B = 2 * QK_B + V_B
MAIN_COLS = Q_A + 2 * KV_A + QKV_B + V_B


def _inproj_kernel(x_ref, nw_ref, w_ref, qa_ref, kva_ref, qkvb_ref, z_ref, gates_ref, w16_ref, wgate16_ref):
    i = pl.program_id(0)

    @pl.when(i == 0)
    def _():
        for j in range(0, MAIN_COLS, INPROJ_STAGE_COLS):
            w16_ref[:, j:j + INPROJ_STAGE_COLS] = w_ref[j:j + INPROJ_STAGE_COLS, :].T.astype(BF16)
        wgate16_ref[...] = jnp.zeros_like(wgate16_ref)
        wgate16_ref[:2 * DN_HEADS, :] = w_ref[MAIN_COLS:, :].astype(BF16)

    @pl.when(i > 0)
    def _():
        xn = _rms(x_ref[...], nw_ref[...]).astype(BF16)
        c0, c1, c2 = Q_A, Q_A + 2 * KV_A, Q_A + 2 * KV_A + QKV_B
        qa_ref[...] = _dot(xn, w16_ref[:, :c0]).astype(BF16)
        kva_ref[...] = _dot(xn, w16_ref[:, c0:c1])
        qkvb_ref[...] = _dot(xn, w16_ref[:, c1:c2])
        z_ref[...] = _dot(xn, w16_ref[:, c2:])
        gates_ref[...] = lax.dot_general(xn, wgate16_ref[...], (((1,), (1,)), ((), ())),
                                         preferred_element_type=F32)


def _inproj(x, norm_w, w_in, layer, e):
    m, d = x.shape
    row = lambda n: _rows_after_prep(ROW_TILE, n)
    w_t = jnp.swapaxes(w_in, 1, 2)
    return pl.pallas_call(
        _inproj_kernel,
        grid=(1 + m // ROW_TILE,),
        in_specs=[row(d), _layer_block((1, d), 3 * layer + 1), _layer_block(w_t.shape[1:], e)],
        out_specs=[row(Q_A), row(2 * KV_A), row(QKV_B), row(V_B), row(GATE_COLS)],
        out_shape=[jax.ShapeDtypeStruct((m, Q_A), BF16),
                   jax.ShapeDtypeStruct((m, 2 * KV_A), F32),
                   jax.ShapeDtypeStruct((m, QKV_B), F32),
                   jax.ShapeDtypeStruct((m, V_B), F32),
                   jax.ShapeDtypeStruct((m, GATE_COLS), F32)],
        scratch_shapes=[pltpu.VMEM((d, MAIN_COLS), BF16), pltpu.VMEM((GATE_COLS, d), BF16)],
        compiler_params=_params("arbitrary"),
        name="mixer_inproj",
    )(x, norm_w.reshape(-1, 1, d), w_t)


def _alibi_slopes(n_heads):
    return [float(2.0 ** (-8.0 * (h + 1) / n_heads)) for h in range(n_heads)]


ATTN_Q_TILE = 512


def _attn_kernel(sinks_ref, q_ref, kv_ref, kv_prev_ref, o_ref, *, sink_row):
    t = pl.program_id(1)
    half = lax.broadcasted_iota(jnp.int32, (1, V7X_LANES), 1) < HEAD_DIM
    kv = jnp.concatenate([kv_prev_ref[...], kv_ref[...]], axis=0)

    def placed(tile):
        swapped = pltpu.roll(tile, HEAD_DIM, axis=1)
        keep_lo = lambda x: jnp.where(half, x, 0.0).astype(BF16)
        keep_hi = lambda x: jnp.where(half, 0.0, x).astype(BF16)
        return {(0, 0): keep_lo(tile), (1, 1): keep_hi(tile), (0, 1): keep_hi(swapped), (1, 0): keep_lo(swapped)}

    k_at = placed(kv[:, :KV_A])
    v_at = placed(kv[:, KV_A:])

    i = lax.broadcasted_iota(jnp.int32, (ATTN_BLOCK, 2 * ATTN_BLOCK), 0)
    j = lax.broadcasted_iota(jnp.int32, (ATTN_BLOCK, 2 * ATTN_BLOCK), 1)
    dist = i + ATTN_BLOCK - j
    valid = jnp.where(dist >= 0, jnp.where(dist < WINDOW, 1, 0), 0) > 0
    no_prev = jnp.where(j < ATTN_BLOCK, jnp.where(t == 0, 1, 0), 0) > 0
    distf = dist.astype(F32)
    slopes = _alibi_slopes(ATTN_HEADS)
    group = ATTN_HEADS // ATTN_KV_HEADS
    bias = [jnp.where(valid, -slopes[h] * distf, -1e30) for h in range(ATTN_HEADS)]
    q = q_ref[...] * (HEAD_DIM ** -0.5)

    for qb in range(ATTN_Q_TILE // ATTN_BLOCK):
        rows = slice(qb * ATTN_BLOCK, (qb + 1) * ATTN_BLOCK)
        win = slice(qb * ATTN_BLOCK, (qb + 2) * ATTN_BLOCK)
        heads = range(ATTN_HEADS)
        place = [(h // group, h % 2) for h in heads]
        sink = [sinks_ref[sink_row, h] for h in heads]
        s = [lax.dot_general(q[rows, (h // 2) * V7X_LANES:(h // 2 + 1) * V7X_LANES], k_at[place[h]][win],
                             (((1,), (1,)), ((), ())), preferred_element_type=F32)
             + (jnp.where(no_prev, -1e30, bias[h]) if qb == 0 else bias[h]) for h in heads]
        mx = [jnp.maximum(jnp.max(s[h], axis=-1, keepdims=True), sink[h]) for h in heads]
        e = [jnp.exp(s[h] - mx[h]) for h in heads]
        inv = [1.0 / (jnp.sum(e[h], axis=-1, keepdims=True) + jnp.exp(sink[h] - mx[h])) for h in heads]
        o = [_dot(e[h].astype(BF16), v_at[place[h]][win]) * inv[h] for h in heads]
        o_ref[rows, :] = jnp.concatenate(
            [o[2 * pair] + o[2 * pair + 1] for pair in range(ATTN_HEADS // 2)], axis=1).astype(BF16)


def _attention(qa, kva, sinks, e, batch, seq):
    assert KV_A == V7X_LANES and ATTN_KV_HEADS == 2 and seq % ATTN_Q_TILE == 0
    nt = seq // ATTN_Q_TILE
    prev_blocks = ATTN_Q_TILE // ATTN_BLOCK
    row_of = lambda b, t: (b * nt + t, 0)
    return pl.pallas_call(
        functools.partial(_attn_kernel, sink_row=e),
        grid=(batch, nt),
        in_specs=[pl.BlockSpec(memory_space=pltpu.SMEM),
                  pl.BlockSpec((ATTN_Q_TILE, Q_A), row_of),
                  pl.BlockSpec((ATTN_Q_TILE, 2 * KV_A), row_of),
                  pl.BlockSpec((ATTN_BLOCK, 2 * KV_A),
                               lambda b, t: (jnp.maximum((b * nt + t) * prev_blocks - 1, 0), 0))],
        out_specs=pl.BlockSpec((ATTN_Q_TILE, Q_A), row_of),
        out_shape=jax.ShapeDtypeStruct(qa.shape, BF16),
        compiler_params=_params("parallel", "parallel"),
        name="swa_attention",
    )(sinks.astype(F32), qa, kva, kva)


PAIR = V7X_LANES // DN_DK
N_PAIRS = DN_HEADS // PAIR


def _split2(x):
    hi = x.astype(BF16)
    return hi, (x - hi.astype(F32)).astype(BF16)


def _split3(x):
    hi = x.astype(BF16)
    rest = x - hi.astype(F32)
    mid = rest.astype(BF16)
    return hi, mid, (rest - mid.astype(F32)).astype(BF16)


def _select_sum(x, onehot, splitter=_split3):
    return sum(_dot(part, onehot) for part in splitter(x))


def _group_sums(x, ones_bd):
    hi, lo = _split2(x)
    cols = []
    for j in range(x.shape[1] // V7X_LANES):
        sl = slice(j * V7X_LANES, (j + 1) * V7X_LANES)
        cols.append(_dot(hi[:, sl], ones_bd) + _dot(lo[:, sl], ones_bd))
    return jnp.concatenate(cols, axis=1)


def _bdot(a, b):
    return lax.dot_general(a, b, (((2,), (1,)), ((0,), (0,))), preferred_element_type=F32)


def _bdot_nt(a, b):
    return lax.dot_general(a, b, (((2,), (2,)), ((0,), (0,))), preferred_element_type=F32)


def _blockdiag(x, same_head):
    return jnp.where(same_head, jnp.concatenate([x, x], axis=1), 0.0).astype(BF16)


def _neumann_lower(low, same_head):
    m = -low
    m16 = m.astype(BF16)
    n = m
    m = _bdot(m16, _blockdiag(m, same_head))
    span = 2
    while span < CHUNK:
        m16 = m.astype(BF16)
        if 2 * span >= CHUNK:
            n = n + m + _bdot(n.astype(BF16), _blockdiag(m, same_head))
        else:
            p = _bdot(jnp.concatenate([n.astype(BF16), m16], axis=1), _blockdiag(m, same_head))
            n = n + m + p[:, :CHUNK]
            m = p[:, CHUNK:]
        span *= 2
    return n


def _deltanet_kernel(qkv_ref, halo_ref, z_ref, gates_ref, convw_ref, alog_ref, dt_ref, normw_ref,
                     o_ref, state_ref, xpad_ref):
    t = pl.program_id(1)
    n_seqs, tt = qkv_ref.shape[0], qkv_ref.shape[1]
    rows = n_seqs * tt
    chunks_per_seq = tt // CHUNK
    n_chunks = n_seqs * chunks_per_seq

    @pl.when(t == 0)
    def _():
        state_ref[...] = jnp.zeros_like(state_ref)
        xpad_ref[:, 0:V7X_SUBLANES, :] = jnp.zeros((n_seqs, V7X_SUBLANES, QKV_B), F32)

    @pl.when(t > 0)
    def _():
        xpad_ref[:, 0:V7X_SUBLANES, :] = halo_ref[...]

    xpad_ref[:, V7X_SUBLANES:, :] = qkv_ref[...]
    first = V7X_SUBLANES - (DN_CONV - 1)
    acc = convw_ref[DN_CONV - 1:DN_CONV, :] * xpad_ref[:, pl.ds(V7X_SUBLANES, tt), :]
    for k in range(DN_CONV - 1):
        acc = acc + convw_ref[k:k + 1, :] * xpad_ref[:, pl.ds(first + k, tt), :]
    xc = _silu(acc).reshape(rows, QKV_B)

    def iota2(shape, axis):
        return lax.broadcasted_iota(jnp.int32, shape, axis)

    sq = (V7X_LANES, V7X_LANES)
    same_head = (iota2(sq, 0) // DN_DK) == (iota2(sq, 1) // DN_DK)
    bd16 = jnp.where(same_head, 1.0, 0.0).astype(BF16)

    qk = xc[:, :2 * QK_B]
    qk = qk * lax.rsqrt(_group_sums(qk * qk, bd16) + EPS)
    q_all = qk[:, :QK_B] * (DN_DK ** -0.5)
    k_all = qk[:, QK_B:]
    v_all = xc[:, 2 * QK_B:]

    gates = gates_ref[...].reshape(rows, GATE_COLS)
    beta_g = 1.0 / (1.0 + jnp.exp(-gates))
    g_g = -jnp.exp(alog_ref[...]) * _softplus(gates + dt_ref[...])
    tril = jnp.where(iota2((CHUNK, CHUNK), 0) >= iota2((CHUNK, CHUNK), 1), 1.0, 0.0).astype(BF16)
    g_parts = _split3(g_g)
    gc_g = jnp.concatenate(
        [sum(_dot(tril, part[c * CHUNK:(c + 1) * CHUNK]) for part in g_parts) for c in range(n_chunks)], axis=0)
    ex = (GATE_COLS, V_B)
    spread_beta = jnp.where(iota2(ex, 0) == iota2(ex, 1) // DN_DK, 1.0, 0.0).astype(BF16)
    spread_g = jnp.where(iota2(ex, 0) == iota2(ex, 1) // DN_DK + DN_HEADS, 1.0, 0.0).astype(BF16)
    beta_x = _select_sum(beta_g, spread_beta)
    gc_x = _select_sum(gc_g, spread_g)

    egc_x = jnp.exp(gc_x)
    kb_all = k_all * beta_x
    vb_all = v_all * beta_x
    kbe_all = kb_all * egc_x
    qd_all = q_all * egc_x

    units = [(c, p) for c in range(n_chunks) for p in range(N_PAIRS)]
    packed = lambda x: jnp.stack(
        [x[c * CHUNK:(c + 1) * CHUNK, p * V7X_LANES:(p + 1) * V7X_LANES] for c, p in units], axis=0)
    gc = packed(gc_x)
    g_last = gc[:, CHUNK - 1:CHUNK, :]
    k = packed(k_all)
    k_dec_t = jnp.swapaxes(k * jnp.exp(g_last - gc), 1, 2).astype(BF16)
    state_gain = jnp.exp(g_last)

    ii = iota2((CHUNK, V7X_LANES), 0)
    jj = iota2((CHUNK, V7X_LANES), 1) % CHUNK
    causal = (ii >= jj)[None]
    strict = (ii > jj)[None]
    gc_row = jnp.sum(jnp.where((ii == jj)[None], gc, 0.0), axis=1, keepdims=True)
    decay = jnp.where(causal, jnp.exp(jnp.where(causal, gc - gc_row, 0.0)), 0.0)

    k_bd = _blockdiag(k, same_head)
    kq = jnp.concatenate([packed(kb_all).astype(BF16), packed(q_all).astype(BF16)], axis=1)
    kk = _bdot_nt(kq, k_bd)
    low = jnp.where(strict, kk[:, :CHUNK] * decay, 0.0)
    attn = (kk[:, CHUNK:] * decay).astype(BF16)

    n16 = _neumann_lower(low, same_head).astype(BF16)
    vb = packed(vb_all)
    kbe = packed(kbe_all)
    rhs = jnp.concatenate([_blockdiag(vb, same_head), _blockdiag(kbe, same_head)], axis=2)
    uw = _bdot(n16, rhs)
    u = vb + uw[:, :, :V7X_LANES]
    w = kbe + uw[:, :, V7X_LANES:]
    state_lhs = jnp.concatenate([w.astype(BF16), packed(qd_all).astype(BF16)], axis=1)

    low_lanes = iota2((DN_DK, V7X_LANES), 1) < DN_DV
    s = state_ref[...]
    outs = []
    for c in range(chunks_per_seq):
        at_c = lambda x: jnp.concatenate(
            [x[(sq * chunks_per_seq + c) * N_PAIRS:(sq * chunks_per_seq + c + 1) * N_PAIRS]
             for sq in range(n_seqs)], axis=0)
        ps = _bdot(at_c(state_lhs), _blockdiag(s, same_head))
        v_new = at_c(u) - ps[:, :CHUNK]
        outs.append(ps[:, CHUNK:] + _bdot(at_c(attn), _blockdiag(v_new, same_head)))
        upd = _bdot(at_c(k_dec_t), v_new.astype(BF16))
        s = s * at_c(state_gain) + jnp.where(low_lanes, upd[:, :DN_DK], upd[:, DN_DK:])
    state_ref[...] = s

    o_all = jnp.concatenate(
        [jnp.concatenate([outs[c][sq * N_PAIRS + p] for p in range(N_PAIRS)], axis=1)
         for sq in range(n_seqs) for c in range(chunks_per_seq)], axis=0)
    o_all = o_all * lax.rsqrt(_group_sums(o_all * o_all, bd16) * (1.0 / DN_DV) + EPS)
    gated = o_all * normw_ref[...] * _silu(z_ref[...].reshape(rows, V_B))
    o_ref[...] = gated.reshape(n_seqs, tt, V_B).astype(BF16)


def _deltanet(qkvb, z, gates, conv_w, a_log, dt_bias, norm_w, e, batch, seq):
    m = qkvb.shape[0]
    assert batch % DN_SEQS == 0 and seq % DN_SEQ_TILE == 0
    n_even = conv_w.shape[0]
    halo_blocks = DN_SEQ_TILE // V7X_SUBLANES
    per_seq = lambda a: a.reshape(batch, seq, a.shape[-1])
    tile = lambda n: pl.BlockSpec((DN_SEQS, DN_SEQ_TILE, n), lambda b, t: (b, t, 0))
    pad_row = lambda p: jnp.pad(p.astype(F32), ((0, 0), (DN_HEADS, GATE_COLS - 2 * DN_HEADS))).reshape(
        n_even, 1, GATE_COLS)
    out = pl.pallas_call(
        _deltanet_kernel,
        grid=(batch // DN_SEQS, seq // DN_SEQ_TILE),
        in_specs=[tile(QKV_B),
                  pl.BlockSpec((DN_SEQS, V7X_SUBLANES, QKV_B),
                               lambda b, t: (b, jnp.maximum(t * halo_blocks - 1, 0), 0)),
                  tile(V_B),
                  tile(GATE_COLS),
                  _layer_block((DN_CONV, QKV_B), e),
                  _layer_block((1, GATE_COLS), e), _layer_block((1, GATE_COLS), e),
                  _layer_block((1, V_B), e)],
        out_specs=tile(V_B),
        out_shape=jax.ShapeDtypeStruct((batch, seq, V_B), BF16),
        scratch_shapes=[pltpu.VMEM((DN_SEQS * N_PAIRS, DN_DK, V7X_LANES), F32),
                        pltpu.VMEM((DN_SEQS, DN_SEQ_TILE + V7X_SUBLANES, QKV_B), F32)],
        compiler_params=_params("parallel", "arbitrary"),
        name="gated_deltanet",
    )(per_seq(qkvb), per_seq(qkvb), per_seq(z), per_seq(gates), conv_w, pad_row(a_log), pad_row(dt_bias),
      jnp.tile(norm_w.astype(F32), (1, DN_HEADS)).reshape(n_even, 1, V_B))
    return out.reshape(m, V_B)


def _outproj_kernel(x_ref, att_ref, dn_ref, w_ref, o_ref, w16_ref):
    i = pl.program_id(0)

    @pl.when(i == 0)
    def _():
        w16_ref[...] = w_ref[...].astype(BF16)

    @pl.when(i > 0)
    def _():
        o_ref[...] = (x_ref[...] + _dot(att_ref[...], w16_ref[:Q_A, :])
                      + _dot(dn_ref[...], w16_ref[Q_A:, :]))


def _outproj(x, att, dn, w_out, e):
    m, d = x.shape
    row = lambda n: _rows_after_prep(ROW_TILE, n)
    return pl.pallas_call(
        _outproj_kernel,
        grid=(1 + m // ROW_TILE,),
        in_specs=[row(d), row(Q_A), row(V_B), _layer_block(w_out.shape[1:], e)],
        out_specs=row(d),
        out_shape=jax.ShapeDtypeStruct((m, d), F32),
        scratch_shapes=[pltpu.VMEM(w_out.shape[1:], BF16)],
        compiler_params=_params("arbitrary"),
        name="mixer_outproj",
    )(x, att, dn, w_out)


def _glu_kernel(x_ref, nw_ref, w_ref, b_ref, o_ref, w16_ref):
    i = pl.program_id(0)

    @pl.when(i == 0)
    def _():
        w16_ref[...] = w_ref[...].astype(BF16)

    @pl.when(i > 0)
    def _():
        xn = _rms(x_ref[...], nw_ref[...]).astype(BF16)
        c = o_ref.shape[1]
        for j in range(0, c, GLU_COL_TILE):
            a = _dot(xn, w16_ref[:, j:j + GLU_COL_TILE]) + b_ref[:, j:j + GLU_COL_TILE]
            g = _dot(xn, w16_ref[:, c + j:c + j + GLU_COL_TILE]) + b_ref[:, c + j:c + j + GLU_COL_TILE]
            o_ref[:, j:j + GLU_COL_TILE] = a * (1.0 / (1.0 + jnp.exp(-g)))


def _glu(x, norm_w, w, b, layer, ci):
    m, d = x.shape
    c = w.shape[2] // 2
    row = lambda n: _rows_after_prep(ROW_TILE, n)
    return pl.pallas_call(
        _glu_kernel,
        grid=(1 + m // ROW_TILE,),
        in_specs=[row(d), _layer_block((1, d), 3 * layer + 1), _layer_block(w.shape[1:], ci),
                  _layer_block((1, 2 * c), ci)],
        out_specs=row(c),
        out_shape=jax.ShapeDtypeStruct((m, c), F32),
        scratch_shapes=[pltpu.VMEM(w.shape[1:], BF16)],
        compiler_params=_params("arbitrary"),
        name="conformer_glu",
    )(x, norm_w.reshape(-1, 1, d), w, b.reshape(b.shape[0], 1, 2 * c))


CONV_HALO = 32


def _dwconv_kernel(x_ref, u_ref, halo_ref, wdw_ref, bdw_ref, lnw_ref, lnb_ref, w2_ref, b2_ref,
                   o_ref, upad_ref, shift_ref, w2_16_ref, *, width, tiles_per_seq):
    i = pl.program_id(0)

    @pl.when(i == 0)
    def _():
        w2_16_ref[...] = w2_ref[...].astype(BF16)

    @pl.when(i > 0)
    def _():
        _dwconv_tile(x_ref, u_ref, halo_ref, wdw_ref, bdw_ref, lnw_ref, lnb_ref, w2_16_ref, b2_ref,
                     o_ref, upad_ref, shift_ref, width=width, first_of_seq=(i - 1) % tiles_per_seq == 0)


def _dwconv_tile(x_ref, u_ref, halo_ref, wdw_ref, bdw_ref, lnw_ref, lnb_ref, w2_ref, b2_ref,
                 o_ref, upad_ref, shift_ref, *, width, first_of_seq):
    tt = u_ref.shape[0]

    @pl.when(first_of_seq)
    def _():
        upad_ref[0:CONV_HALO, :] = jnp.zeros((CONV_HALO, upad_ref.shape[1]), F32)

    @pl.when(jnp.logical_not(first_of_seq))
    def _():
        upad_ref[0:CONV_HALO, :] = halo_ref[...]

    upad_ref[CONV_HALO:, :] = u_ref[...]
    span = tt + CONV_HALO - V7X_SUBLANES
    for s in range(1, V7X_SUBLANES):
        shift_ref[s - 1, 0:span, :] = upad_ref[pl.ds(s, span), :]
    first = CONV_HALO - (width - 1)

    def tap_rows(k):
        tile_row, s = divmod(first + k, V7X_SUBLANES)
        start = tile_row * V7X_SUBLANES
        return upad_ref[pl.ds(start, tt), :] if s == 0 else shift_ref[s - 1, pl.ds(start, tt), :]

    acc = bdw_ref[...]
    for k in range(width):
        acc = acc + wdw_ref[k:k + 1, :] * tap_rows(k)
    mu = jnp.mean(acc, axis=-1, keepdims=True)
    xc = acc - mu
    y = xc * lax.rsqrt(jnp.mean(xc * xc, axis=-1, keepdims=True) + EPS) * lnw_ref[...] + lnb_ref[...]
    y = _silu(y).astype(BF16)
    o_ref[...] = x_ref[...] + _dot(y, w2_ref[...]) + b2_ref[...]


def _dwconv(x, u, w_dw, b_dw, ln_w, ln_b, w2, b2, ci, seq):
    m, d = x.shape
    c = u.shape[1]
    width = w_dw.shape[1]
    halo_blocks = SEQ_TILE // CONV_HALO
    vec = lambda p: p.reshape(p.shape[0], 1, p.shape[1])
    return pl.pallas_call(
        functools.partial(_dwconv_kernel, width=width, tiles_per_seq=seq // SEQ_TILE),
        grid=(1 + m // SEQ_TILE,),
        in_specs=[_rows_after_prep(SEQ_TILE, d),
                  _rows_after_prep(SEQ_TILE, c),
                  pl.BlockSpec((CONV_HALO, c), lambda i: (jnp.maximum((i - 1) * halo_blocks - 1, 0), 0)),
                  _layer_block((width, c), ci), _layer_block((1, c), ci), _layer_block((1, c), ci),
                  _layer_block((1, c), ci), _layer_block((c, d), ci), _layer_block((1, d), ci)],
        out_specs=_rows_after_prep(SEQ_TILE, d),
        out_shape=jax.ShapeDtypeStruct((m, d), F32),
        scratch_shapes=[pltpu.VMEM((SEQ_TILE + CONV_HALO, c), F32),
                        pltpu.VMEM((V7X_SUBLANES - 1, SEQ_TILE + CONV_HALO, c), F32),
                        pltpu.VMEM((c, d), BF16)],
        compiler_params=_params("arbitrary"),
        name="conformer_dwconv",
    )(x, u, u, w_dw, vec(b_dw), vec(ln_w), vec(ln_b), w2, vec(b2))


def kernel(x, norm_w, ffn_w_gate, ffn_w_up, ffn_w_down, mix_w_in, dn_conv_w, attn_sinks, dn_a_log, dn_dt_bias, dn_norm_w, mix_w_out, conv_w_pw1, conv_b_pw1, conv_w_dw, conv_b_dw, conv_ln_w, conv_ln_b, conv_w_pw2, conv_b_pw2, final_norm_w):
    batch, seq, d = x.shape
    depth = norm_w.shape[0]
    assert seq % SEQ_TILE == 0 and (batch * seq) % ROW_TILE == 0 and (batch * seq) % FFN_ROW_TILE == 0
    assert ffn_w_gate.shape[-1] % FFN_COL_TILE == 0
    assert conv_w_dw.shape[1] <= CONV_HALO + 1 and WINDOW <= ATTN_BLOCK
    assert mix_w_in.shape[2] == MAIN_COLS + 2 * DN_HEADS
    h = x.reshape(batch * seq, d)
    for layer in range(depth):
        h = _ffn(h, norm_w, ffn_w_gate, ffn_w_up, ffn_w_down, layer, 0)
        if layer % 2 == 0:
            e = layer // 2
            qa, kva, qkvb, z, gates = _inproj(h, norm_w, mix_w_in, layer, e)
            att = _attention(qa, kva, attn_sinks, e, batch, seq)
            dn = _deltanet(qkvb, z, gates, dn_conv_w, dn_a_log, dn_dt_bias, dn_norm_w, e, batch, seq)
            h = _outproj(h, att, dn, mix_w_out, e)
        else:
            c = layer // 2
            u = _glu(h, norm_w, conv_w_pw1, conv_b_pw1, layer, c)
            h = _dwconv(h, u, conv_w_dw, conv_b_dw, conv_ln_w, conv_ln_b, conv_w_pw2, conv_b_pw2, c, seq)
        last = layer == depth - 1
        h = _ffn(h, norm_w, ffn_w_gate, ffn_w_up, ffn_w_down, layer, 1,
                 final_w=final_norm_w if last else None)
    return h.reshape(batch, seq, d)
```
